```python
import jax, jax.numpy as jnp
from jax import lax
import numpy as np

D_MODEL = 1024
BATCH = 8
SEQ = 2048
DEPTH = 4
DEC_BATCH = 128
DEC_SEQ = 4
PAST_LEN = 8192
PAGE_SIZE = 128

D_CONV_A = D_MODEL // 2
D_CONV_B = D_MODEL // 2
CONV_A_WIDTH = 31
CONV_B_WIDTH = 3
N_HEADS = 8
Q_LORA = 384
KV_LORA = 256
QK_NOPE = 128
QK_ROPE = 64
V_DIM = 128
ROPE_THETA = 10000.0
SM_SCALE = (QK_NOPE + QK_ROPE) ** -0.5
Q_BLOCK = 128
D_FF = -(-8 * D_MODEL // (3 * 256)) * 256
N_CONV_LAYERS = (DEPTH + 1) // 2
N_MLA_LAYERS = DEPTH // 2
RMS_EPS = 1e-6
LN_EPS = 1e-5

kernel_name = "hybrid_conformer_shortconv_mla_decoder_step"


def rmsnorm(x, g):
    xf = x.astype(jnp.float32)
    y = xf * lax.rsqrt(jnp.mean(xf * xf, axis=-1, keepdims=True) + RMS_EPS)
    return (y * g.astype(jnp.float32)).astype(x.dtype)


def layernorm(x, g, b):
    xf = x.astype(jnp.float32)
    mu = jnp.mean(xf, axis=-1, keepdims=True)
    xc = xf - mu
    var = jnp.mean(xc * xc, axis=-1, keepdims=True)
    return (xc * lax.rsqrt(var + LN_EPS) * g.astype(jnp.float32) + b.astype(jnp.float32)).astype(x.dtype)


def rope(x, pos):
    half = QK_ROPE // 2
    inv_freq = 1.0 / (ROPE_THETA ** (jnp.arange(half, dtype=jnp.float32) * (2.0 / QK_ROPE)))
    ang = pos.astype(jnp.float32)[:, None] * inv_freq[None, :]
    shape = (1, pos.shape[0]) + (1,) * (x.ndim - 3) + (half,)
    cos = jnp.cos(ang).reshape(shape)
    sin = jnp.sin(ang).reshape(shape)
    xf = x.astype(jnp.float32)
    x1, x2 = xf[..., :half], xf[..., half:]
    return jnp.concatenate([x1 * cos - x2 * sin, x1 * sin + x2 * cos], axis=-1).astype(x.dtype)


def causal_dwconv(xpad, w):
    return lax.conv_general_dilated(
        xpad, w[:, None, :].astype(xpad.dtype), window_strides=(1,), padding="VALID",
        dimension_numbers=("NWC", "WIO", "NWC"), feature_group_count=xpad.shape[-1])


def swiglu_ffn(h, wg, wu, wd):
    return (jax.nn.silu(h @ wg) * (h @ wu)) @ wd


def conv_mixer(h, state_a, state_b, w_in, wa, ba, lng, lnb, wb, w_out):
    u = h @ w_in
    a_val, a_gate, b_gate, c_gate, b_in = jnp.split(
        u, [D_CONV_A, 2 * D_CONV_A, 2 * D_CONV_A + D_CONV_B, 2 * D_CONV_A + 2 * D_CONV_B], axis=-1)
    glu = a_val * jax.nn.sigmoid(a_gate)
    apad = jnp.concatenate([state_a.astype(glu.dtype), glu], axis=1)
    a = jax.nn.silu(layernorm(causal_dwconv(apad, wa) + ba, lng, lnb))
    z = c_gate * b_in
    zpad = jnp.concatenate([state_b.astype(z.dtype), z], axis=1)
    b = b_gate * causal_dwconv(zpad, wb)
    out = jnp.concatenate([a, b], axis=-1) @ w_out
    return out, apad[:, -(CONV_A_WIDTH - 1):], zpad[:, -(CONV_B_WIDTH - 1):]


def mla_project(h, pos, w_in, qg, w_uq, kvg, w_uk):
    B, S, _ = h.shape
    u = h @ w_in
    cq, ckv, kr = jnp.split(u, [Q_LORA, Q_LORA + KV_LORA], axis=-1)
    q = (rmsnorm(cq, qg) @ w_uq).reshape(B, S, N_HEADS, QK_NOPE + QK_ROPE)
    q_nope, q_rope = q[..., :QK_NOPE], rope(q[..., QK_NOPE:], pos)
    c_kv = rmsnorm(ckv, kvg)
    k_rope = rope(kr, pos)
    q_lat = jnp.einsum('bshn,lhn->bshl', q_nope, w_uk)
    return q_lat, q_rope, c_kv, k_rope


def mla_attend(q_lat, q_rope, c_kv, k_rope, q_pos, k_pos):
    s = (jnp.einsum('bqhl,bkl->bhqk', q_lat, c_kv)
         + jnp.einsum('bqhr,bkr->bhqk', q_rope, k_rope)).astype(jnp.float32) * SM_SCALE
    mask = k_pos[None, :] <= q_pos[:, None]
    s = jnp.where(mask[None, None], s, -jnp.inf)
    p = jax.nn.softmax(s, axis=-1).astype(c_kv.dtype)
    return jnp.einsum('bhqk,bkl->bqhl', p, c_kv)


def attend_prompt(q_lat, q_rope, c_kv, k_rope):
    B, S, H, L = q_lat.shape
    nb = S // Q_BLOCK
    qlb = q_lat.reshape(B, nb, Q_BLOCK, H, L).transpose(1, 0, 2, 3, 4)
    qrb = q_rope.reshape(B, nb, Q_BLOCK, H, QK_ROPE).transpose(1, 0, 2, 3, 4)
    qpos = jnp.arange(S).reshape(nb, Q_BLOCK)
    kpos = jnp.arange(S)
    out = lax.map(lambda a: mla_attend(a[0], a[1], c_kv, k_rope, a[2], kpos), (qlb, qrb, qpos))
    return out.transpose(1, 0, 2, 3, 4).reshape(B, S, H, L)


def mla_output(o_lat, w_uv, w_o):
    B, Q = o_lat.shape[:2]
    o = jnp.einsum('bqhl,lhv->bqhv', o_lat, w_uv).reshape(B, Q, N_HEADS * V_DIM)
    return o @ w_o


def setup_inputs(seed: int = 0) -> dict:
    key = jax.random.key(seed)
    ks = iter(jax.random.split(key, 40))
    f32 = jnp.float32

    def nrm(shape, scale=1.0):
        return jax.random.normal(next(ks), shape, f32) * scale

    def gain(shape):
        return 1.0 + 0.05 * jax.random.normal(next(ks), shape, f32)

    n_pages = PAST_LEN // PAGE_SIZE
    n_used = DEC_BATCH * n_pages
    n_pool = n_used + n_used // 4
    page_table = jax.random.permutation(next(ks), n_pool)[:n_used].reshape(DEC_BATCH, n_pages).astype(jnp.int32)
    d_in_conv = 2 * D_CONV_A + 3 * D_CONV_B
    return {
        "x_prompt": nrm((BATCH, SEQ, D_MODEL)),
        "x_sample": nrm((DEC_BATCH, DEC_SEQ, D_MODEL)),
        "state_conv_a": nrm((N_CONV_LAYERS, DEC_BATCH, CONV_A_WIDTH - 1, D_CONV_A), 0.5),
        "state_conv_b": nrm((N_CONV_LAYERS, DEC_BATCH, CONV_B_WIDTH - 1, D_CONV_B), 0.5),
        "cache_kv_latent": nrm((N_MLA_LAYERS, n_pool, PAGE_SIZE, KV_LORA)),
        "cache_k_rope": nrm((N_MLA_LAYERS, n_pool, PAGE_SIZE, QK_ROPE)),
        "page_table": page_table,
        "mix_pre_gain": gain((DEPTH, D_MODEL)),
        "mix_post_gain": gain((DEPTH, D_MODEL)),
        "ffn_pre_gain": gain((DEPTH, D_MODEL)),
        "ffn_post_gain": gain((DEPTH, D_MODEL)),
        "w_ffn_gate": nrm((DEPTH, D_MODEL, D_FF), D_MODEL ** -0.5),
        "w_ffn_up": nrm((DEPTH, D_MODEL, D_FF), D_MODEL ** -0.5),
        "w_ffn_down": nrm((DEPTH, D_FF, D_MODEL), D_FF ** -0.5),
        "w_in_conv": nrm((N_CONV_LAYERS, D_MODEL, d_in_conv), D_MODEL ** -0.5),
        "conv_a_w": nrm((N_CONV_LAYERS, CONV_A_WIDTH, D_CONV_A), CONV_A_WIDTH ** -0.5),
        "conv_a_b": nrm((N_CONV_LAYERS, D_CONV_A), 0.02),
        "ln_a_gain": gain((N_CONV_LAYERS, D_CONV_A)),
        "ln_a_bias": nrm((N_CONV_LAYERS, D_CONV_A), 0.02),
        "conv_b_w": nrm((N_CONV_LAYERS, CONV_B_WIDTH, D_CONV_B), CONV_B_WIDTH ** -0.5),
        "w_out_conv": nrm((N_CONV_LAYERS, D_CONV_A + D_CONV_B, D_MODEL), (D_CONV_A + D_CONV_B) ** -0.5),
        "w_in_mla": nrm((N_MLA_LAYERS, D_MODEL, Q_LORA + KV_LORA + QK_ROPE), D_MODEL ** -0.5),
        "q_norm_gain": gain((N_MLA_LAYERS, Q_LORA)),
        "w_uq": nrm((N_MLA_LAYERS, Q_LORA, N_HEADS * (QK_NOPE + QK_ROPE)), Q_LORA ** -0.5),
        "kv_norm_gain": gain((N_MLA_LAYERS, KV_LORA)),
        "w_uk": nrm((N_MLA_LAYERS, KV_LORA, N_HEADS, QK_NOPE), KV_LORA ** -0.5),
        "w_uv": nrm((N_MLA_LAYERS, KV_LORA, N_HEADS, V_DIM), KV_LORA ** -0.5),
        "w_o_mla": nrm((N_MLA_LAYERS, N_HEADS * V_DIM, D_MODEL), (N_HEADS * V_DIM) ** -0.5),
    }


def reference(x_prompt, x_sample, state_conv_a, state_conv_b, cache_kv_latent, cache_k_rope, page_table,
              mix_pre_gain, mix_post_gain, ffn_pre_gain, ffn_post_gain, w_ffn_gate, w_ffn_up, w_ffn_down,
              w_in_conv, conv_a_w, conv_a_b, ln_a_gain, ln_a_bias, conv_b_w, w_out_conv,
              w_in_mla, q_norm_gain, w_uq, kv_norm_gain, w_uk, w_uv, w_o_mla):
    Bp, Sp = x_prompt.shape[:2]
    Bs, Ss = x_sample.shape[:2]
    past_len = page_table.shape[1] * cache_kv_latent.shape[2]
    pos_p = jnp.arange(Sp)
    pos_s = past_len + jnp.arange(Ss)
    k_pos_s = jnp.arange(past_len + Ss)
    yp, ys = x_prompt, x_sample
    ca_p, ca_s, cb_p, cb_s = [], [], [], []
    kv_p, kv_s, kr_p, kr_s = [], [], [], []
    for layer in range(DEPTH):
        hp = rmsnorm(yp, mix_pre_gain[layer])
        hs = rmsnorm(ys, mix_pre_gain[layer])
        if layer % 2 == 0:
            i = layer // 2
            prm = (w_in_conv[i], conv_a_w[i], conv_a_b[i], ln_a_gain[i], ln_a_bias[i], conv_b_w[i], w_out_conv[i])
            zero_a = jnp.zeros((Bp, CONV_A_WIDTH - 1, D_CONV_A), hp.dtype)
            zero_b = jnp.zeros((Bp, CONV_B_WIDTH - 1, D_CONV_B), hp.dtype)
            op, na_p, nb_p = conv_mixer(hp, zero_a, zero_b, *prm)
            os_, na_s, nb_s = conv_mixer(hs, state_conv_a[i], state_conv_b[i], *prm)
            ca_p.append(na_p); ca_s.append(na_s); cb_p.append(nb_p); cb_s.append(nb_s)
        else:
            j = layer // 2
            prm = (w_in_mla[j], q_norm_gain[j], w_uq[j], kv_norm_gain[j], w_uk[j])
            ql_p, qr_p, c_p, r_p = mla_project(hp, pos_p, *prm)
            op = mla_output(attend_prompt(ql_p, qr_p, c_p, r_p), w_uv[j], w_o_mla[j])
            ql_s, qr_s, c_s, r_s = mla_project(hs, pos_s, *prm)
            past_c = cache_kv_latent[j, page_table].reshape(Bs, past_len, KV_LORA).astype(c_s.dtype)
            past_r = cache_k_rope[j, page_table].reshape(Bs, past_len, QK_ROPE).astype(r_s.dtype)
            c_all = jnp.concatenate([past_c, c_s], axis=1)
            r_all = jnp.concatenate([past_r, r_s], axis=1)
            os_ = mla_output(mla_attend(ql_s, qr_s, c_all, r_all, pos_s, k_pos_s), w_uv[j], w_o_mla[j])
            kv_p.append(c_p); kv_s.append(c_s); kr_p.append(r_p); kr_s.append(r_s)
        yp = yp + rmsnorm(op, mix_post_gain[layer])
        ys = ys + rmsnorm(os_, mix_post_gain[layer])
        fp = swiglu_ffn(rmsnorm(yp, ffn_pre_gain[layer]), w_ffn_gate[layer], w_ffn_up[layer], w_ffn_down[layer])
        fs = swiglu_ffn(rmsnorm(ys, ffn_pre_gain[layer]), w_ffn_gate[layer], w_ffn_up[layer], w_ffn_down[layer])
        yp = yp + rmsnorm(fp, ffn_post_gain[layer])
        ys = ys + rmsnorm(fs, ffn_post_gain[layer])
    new_conv_a_prompt = jnp.stack(ca_p)
    new_conv_a_sample = jnp.stack(ca_s)
    new_conv_b_prompt = jnp.stack(cb_p)
    new_conv_b_sample = jnp.stack(cb_s)
    new_kv_latent_prompt = jnp.stack(kv_p)
    new_kv_latent_sample = jnp.stack(kv_s)
    new_k_rope_prompt = jnp.stack(kr_p)
    new_k_rope_sample = jnp.stack(kr_s)
    return (yp, ys, new_conv_a_prompt, new_conv_a_sample, new_conv_b_prompt, new_conv_b_sample,
            new_kv_latent_prompt, new_kv_latent_sample, new_k_rope_prompt, new_k_rope_sample)
```

```python
import functools

import jax
import jax.numpy as jnp
from jax import lax
from jax.experimental import pallas as pl
from jax.experimental.pallas import tpu as pltpu

F32 = jnp.float32
BF16 = jnp.bfloat16

ROPE_THETA = 10000.0
RMS_EPS = 1e-6
LN_EPS = 1e-5

V7X_VMEM_LIMIT_BYTES = 56 * 1024 * 1024
SUBLANES = 8

FFN_TM = 1024
FFN_TF = 256
CONV_TS = 512
CONV_RC = 32
CONV_HIST = 32
CONV_SAMPLE_NB = 64
PROJ_TM = 512
ATTN_TQ = 256
PAGES_PER_STEP = 8


def _params(*sem):
    return pltpu.CompilerParams(dimension_semantics=sem, vmem_limit_bytes=V7X_VMEM_LIMIT_BYTES)


def _rmsnorm(x, g):
    return x * lax.rsqrt(jnp.mean(x * x, axis=-1, keepdims=True) + RMS_EPS) * g


def _layernorm(x, g, b):
    mu = jnp.mean(x, axis=-1, keepdims=True)
    xc = x - mu
    var = jnp.mean(xc * xc, axis=-1, keepdims=True)
    return xc * lax.rsqrt(var + LN_EPS) * g + b


def _silu(x):
    return x * jax.nn.sigmoid(x)


def _dot(a, b):
    return jnp.dot(a, b, preferred_element_type=F32)


def _dot_nt(a, b):
    return lax.dot_general(a, b, (((1,), (1,)), ((), ())), preferred_element_type=F32)


def _ffn_body(x_ref, pre_ref, post_ref, wg_ref, wu_ref, wd_ref, o_ref, h_scr, acc_scr):
    k = pl.program_id(1)

    @pl.when(k == 0)
    def _():
        h_scr[...] = _rmsnorm(x_ref[...], pre_ref[...]).astype(BF16)

    h = h_scr[...]
    act = (_silu(_dot(h, wg_ref[...])) * _dot(h, wu_ref[...])).astype(BF16)
    d = _dot(act, wd_ref[...])

    @pl.when(k == 0)
    def _():
        acc_scr[...] = d

    @pl.when(k > 0)
    def _():
        acc_scr[...] += d

    @pl.when(k == pl.num_programs(1) - 1)
    def _():
        o_ref[...] = x_ref[...] + _rmsnorm(acc_scr[...], post_ref[...])


def _ffn(x, layer, pre, post, wg, wu, wd):
    T, D = x.shape
    F = wg.shape[-1]
    tm = min(FFN_TM, T)
    gain_spec = pl.BlockSpec((None, 1, D), lambda i, k: (layer, 0, 0))
    return pl.pallas_call(
        _ffn_body,
        grid=(T // tm, F // FFN_TF),
        in_specs=[
            pl.BlockSpec((tm, D), lambda i, k: (i, 0)),
            gain_spec, gain_spec,
            pl.BlockSpec((None, D, FFN_TF), lambda i, k: (layer, 0, k)),
            pl.BlockSpec((None, D, FFN_TF), lambda i, k: (layer, 0, k)),
            pl.BlockSpec((None, FFN_TF, D), lambda i, k: (layer, k, 0)),
        ],
        out_specs=pl.BlockSpec((tm, D), lambda i, k: (i, 0)),
        out_shape=jax.ShapeDtypeStruct((T, D), F32),
        scratch_shapes=[pltpu.VMEM((tm, D), BF16), pltpu.VMEM((tm, D), F32)],
        compiler_params=_params("parallel", "arbitrary"),
        name="ffn",
    )(x, pre, post, wg, wu, wd)


def _conv_prompt_body(x_ref, pre_ref, post_ref, win_ref, wa_ref, ba_ref, lng_ref, lnb_ref, wb_ref, wout_ref,
                      o_ref, sa_ref, sb_ref, u_scr, apad_scr, zpad_scr, cat_scr):
    t = pl.program_id(1)
    ts = x_ref.shape[0]
    ca = apad_scr.shape[1]
    wa_n = wa_ref.shape[0]
    wb_n = wb_ref.shape[0]

    @pl.when(t == 0)
    def _():
        apad_scr[0:CONV_HIST, :] = jnp.zeros((CONV_HIST, ca), F32)
        zpad_scr[0:SUBLANES, :] = jnp.zeros((SUBLANES, ca), F32)

    x = x_ref[...]
    h = _rmsnorm(x, pre_ref[...]).astype(BF16)
    u_scr[...] = _dot(h, win_ref[...])
    apad_scr[CONV_HIST:CONV_HIST + ts, :] = u_scr[:, 0:ca] * jax.nn.sigmoid(u_scr[:, ca:2 * ca])
    zpad_scr[SUBLANES:SUBLANES + ts, :] = u_scr[:, 3 * ca:4 * ca] * u_scr[:, 4 * ca:5 * ca]

    a0 = CONV_HIST - (wa_n - 1)
    z0 = SUBLANES - (wb_n - 1)
    for c in range(ts // CONV_RC):
        r0 = c * CONV_RC
        acc = jnp.broadcast_to(ba_ref[...], (CONV_RC, ca))
        for k in range(wa_n):
            acc = acc + wa_ref[k:k + 1, :] * apad_scr[r0 + a0 + k:r0 + a0 + k + CONV_RC, :]
        a = _silu(_layernorm(acc, lng_ref[...], lnb_ref[...]))
        cat_scr[r0:r0 + CONV_RC, 0:ca] = a.astype(BF16)
        cb = wb_ref[0:1, :] * zpad_scr[r0 + z0:r0 + z0 + CONV_RC, :]
        for k in range(1, wb_n):
            cb = cb + wb_ref[k:k + 1, :] * zpad_scr[r0 + z0 + k:r0 + z0 + k + CONV_RC, :]
        b = u_scr[r0:r0 + CONV_RC, 2 * ca:3 * ca] * cb
        cat_scr[r0:r0 + CONV_RC, ca:2 * ca] = b.astype(BF16)

    out = _dot(cat_scr[...], wout_ref[...])
    o_ref[...] = x + _rmsnorm(out, post_ref[...])

    @pl.when(t == pl.num_programs(1) - 1)
    def _():
        sa_ref[...] = apad_scr[CONV_HIST + ts - (wa_n - 1):CONV_HIST + ts, :]
        sb_ref[...] = zpad_scr[SUBLANES + ts - (wb_n - 1):SUBLANES + ts, :]

    apad_scr[0:CONV_HIST, :] = apad_scr[ts:ts + CONV_HIST, :]
    zpad_scr[0:SUBLANES, :] = zpad_scr[ts:ts + SUBLANES, :]


def _conv_prompt(x, i, layer, batch, pre, post, win, wa, ba, lng, lnb, wb, wout):
    T, D = x.shape
    S = T // batch
    ts = CONV_TS
    nt = S // ts
    ca = wa.shape[-1]
    wa_n, wb_n = wa.shape[1], wb.shape[1]
    assert wa_n - 1 <= CONV_HIST and wb_n - 1 <= SUBLANES
    gain_spec = pl.BlockSpec((None, 1, D), lambda b, t: (layer, 0, 0))
    vec_spec = pl.BlockSpec((None, 1, ca), lambda b, t: (i, 0, 0))
    return pl.pallas_call(
        _conv_prompt_body,
        grid=(batch, nt),
        in_specs=[
            pl.BlockSpec((ts, D), lambda b, t: (b * nt + t, 0)),
            gain_spec, gain_spec,
            pl.BlockSpec((None, D, 5 * ca), lambda b, t: (i, 0, 0)),
            pl.BlockSpec((None, wa_n, ca), lambda b, t: (i, 0, 0)),
            vec_spec, vec_spec, vec_spec,
            pl.BlockSpec((None, wb_n, ca), lambda b, t: (i, 0, 0)),
            pl.BlockSpec((None, 2 * ca, D), lambda b, t: (i, 0, 0)),
        ],
        out_specs=[
            pl.BlockSpec((ts, D), lambda b, t: (b * nt + t, 0)),
            pl.BlockSpec((None, wa_n - 1, ca), lambda b, t: (b, 0, 0)),
            pl.BlockSpec((None, wb_n - 1, ca), lambda b, t: (b, 0, 0)),
        ],
        out_shape=[
            jax.ShapeDtypeStruct((T, D), F32),
            jax.ShapeDtypeStruct((batch, wa_n - 1, ca), F32),
            jax.ShapeDtypeStruct((batch, wb_n - 1, ca), F32),
        ],
        scratch_shapes=[
            pltpu.VMEM((ts, 5 * ca), F32),
            pltpu.VMEM((CONV_HIST + ts, ca), F32),
            pltpu.VMEM((SUBLANES + ts, ca), F32),
            pltpu.VMEM((ts, 2 * ca), BF16),
        ],
        compiler_params=_params("parallel", "arbitrary"),
        name="conv_prompt",
    )(x, pre, post, win, wa, ba, lng, lnb, wb, wout)


def _conv_sample_body(x_ref, sa_ref, sb_ref, pre_ref, post_ref, win_ref, wa_ref, ba_ref, lng_ref, lnb_ref,
                      wb_ref, wout_ref, o_ref, nsa_ref, nsb_ref, u_scr, glu_scr, z_scr, cat_scr):
    tt, nb, d = x_ref.shape
    ca = glu_scr.shape[-1]
    wa_n = wa_ref.shape[0]
    wb_n = wb_ref.shape[0]
    x = x_ref[...].reshape(tt * nb, d)
    h = _rmsnorm(x, pre_ref[...]).astype(BF16)
    u_scr[...] = _dot(h, win_ref[...])
    for t in range(tt):
        rows = slice(t * nb, (t + 1) * nb)
        glu_scr[t] = u_scr[rows, 0:ca] * jax.nn.sigmoid(u_scr[rows, ca:2 * ca])
        z_scr[t] = u_scr[rows, 3 * ca:4 * ca] * u_scr[rows, 4 * ca:5 * ca]

    def apad(j, rs):
        return sa_ref[j, rs, :] if j < wa_n - 1 else glu_scr[j - (wa_n - 1), rs, :]

    def zpad(j, rs):
        return sb_ref[j, rs, :] if j < wb_n - 1 else z_scr[j - (wb_n - 1), rs, :]

    for t in range(tt):
        for r in range(nb // CONV_RC):
            rs = slice(r * CONV_RC, (r + 1) * CONV_RC)
            orows = slice(t * nb + r * CONV_RC, t * nb + (r + 1) * CONV_RC)
            acc = jnp.broadcast_to(ba_ref[...], (CONV_RC, ca))
            for k in range(wa_n):
                acc = acc + wa_ref[k:k + 1, :] * apad(t + k, rs)
            a = _silu(_layernorm(acc, lng_ref[...], lnb_ref[...]))
            cat_scr[orows, 0:ca] = a.astype(BF16)
            cb = wb_ref[0:1, :] * zpad(t, rs)
            for k in range(1, wb_n):
                cb = cb + wb_ref[k:k + 1, :] * zpad(t + k, rs)
            cat_scr[orows, ca:2 * ca] = (u_scr[orows, 2 * ca:3 * ca] * cb).astype(BF16)

    out = _dot(cat_scr[...], wout_ref[...])
    o_ref[...] = (x + _rmsnorm(out, post_ref[...])).reshape(tt, nb, d)
    full = slice(0, nb)
    for j in range(wa_n - 1):
        nsa_ref[j] = apad(j + tt, full)
    for j in range(wb_n - 1):
        nsb_ref[j] = zpad(j + tt, full)


def _conv_sample(x, sa, sb, i, layer, pre, post, win, wa, ba, lng, lnb, wb, wout):
    tt, B, D = x.shape
    nb = CONV_SAMPLE_NB
    ca = wa.shape[-1]
    wa_n, wb_n = wa.shape[1], wb.shape[1]
    gain_spec = pl.BlockSpec((None, 1, D), lambda g: (layer, 0, 0))
    vec_spec = pl.BlockSpec((None, 1, ca), lambda g: (i, 0, 0))
    return pl.pallas_call(
        _conv_sample_body,
        grid=(B // nb,),
        in_specs=[
            pl.BlockSpec((tt, nb, D), lambda g: (0, g, 0)),
            pl.BlockSpec((None, wa_n - 1, nb, ca), lambda g: (i, 0, g, 0)),
            pl.BlockSpec((None, wb_n - 1, nb, ca), lambda g: (i, 0, g, 0)),
            gain_spec, gain_spec,
            pl.BlockSpec((None, D, 5 * ca), lambda g: (i, 0, 0)),
            pl.BlockSpec((None, wa_n, ca), lambda g: (i, 0, 0)),
            vec_spec, vec_spec, vec_spec,
            pl.BlockSpec((None, wb_n, ca), lambda g: (i, 0, 0)),
            pl.BlockSpec((None, 2 * ca, D), lambda g: (i, 0, 0)),
        ],
        out_specs=[
            pl.BlockSpec((tt, nb, D), lambda g: (0, g, 0)),
            pl.BlockSpec((wa_n - 1, nb, ca), lambda g: (0, g, 0)),
            pl.BlockSpec((wb_n - 1, nb, ca), lambda g: (0, g, 0)),
        ],
        out_shape=[
            jax.ShapeDtypeStruct((tt, B, D), F32),
            jax.ShapeDtypeStruct((wa_n - 1, B, ca), F32),
            jax.ShapeDtypeStruct((wb_n - 1, B, ca), F32),
        ],
        scratch_shapes=[
            pltpu.VMEM((tt * nb, 5 * ca), F32),
            pltpu.VMEM((tt, nb, ca), F32),
            pltpu.VMEM((tt, nb, ca), F32),
            pltpu.VMEM((tt * nb, 2 * ca), BF16),
        ],
        compiler_params=_params("parallel"),
        name="conv_sample",
    )(x, sa, sb, pre, post, win, wa, ba, lng, lnb, wb, wout)


def _mla_proj_body(x_ref, pre_ref, win_ref, qg_ref, wuqn_ref, wuqr_ref, kvg_ref, wukt_ref,
                   cosq_ref, sinq_ref, cosk_ref, sink_ref,
                   qlat_ref, qrope_ref, ckv_ref, kr_ref, ckvb_ref, krb_ref):
    n_heads, _, kvl = qlat_ref.shape
    rope = qrope_ref.shape[-1]
    half = rope // 2
    ql = qg_ref.shape[-1]
    nope = wuqn_ref.shape[-1] // n_heads

    h = _rmsnorm(x_ref[...], pre_ref[...]).astype(BF16)
    u = _dot(h, win_ref[...])
    cq, ckv, kr = u[:, :ql], u[:, ql:ql + kvl], u[:, ql + kvl:]
    qn = _rmsnorm(cq, qg_ref[...]).astype(BF16)
    q_nope = _dot(qn, wuqn_ref[...])
    q_r = _dot(qn, wuqr_ref[...])
    width = n_heads * rope
    lane = lax.broadcasted_iota(jnp.int32, q_r.shape, 1)
    swapped = jnp.where(lane % rope < half,
                        pltpu.roll(q_r, width - half, 1), pltpu.roll(q_r, half, 1))
    q_rot = q_r * cosq_ref[...] + swapped * sinq_ref[...]
    for hd in range(n_heads):
        q_h = q_nope[:, hd * nope:(hd + 1) * nope].astype(BF16)
        qlat_ref[hd] = _dot(q_h, wukt_ref[hd]).astype(BF16)
        qrope_ref[hd] = q_rot[:, hd * rope:(hd + 1) * rope].astype(BF16)

    c = _rmsnorm(ckv, kvg_ref[...])
    ckv_ref[...] = c
    ckvb_ref[...] = c.astype(BF16)
    kr_swapped = jnp.concatenate([kr[:, half:], kr[:, :half]], axis=-1)
    kr_rot = kr * cosk_ref[...] + kr_swapped * sink_ref[...]
    kr_ref[...] = kr_rot
    krb_ref[...] = kr_rot.astype(BF16)


def _mla_proj(x, j, layer, pre, win, qg, wuqn, wuqr, kvg, wukt, cosq, sinq, cosk, sink):
    T, D = x.shape
    tm = min(PROJ_TM, T)
    n_pos = cosq.shape[0] // tm
    n_heads, nope, kvl = wukt.shape[1:]
    ql = qg.shape[-1]
    rope = cosk.shape[-1]
    din = win.shape[-1]
    tok = lambda w: pl.BlockSpec((tm, w), lambda i: (i, 0))
    pos = lambda w: pl.BlockSpec((tm, w), lambda i: (i % n_pos, 0))
    return pl.pallas_call(
        _mla_proj_body,
        grid=(T // tm,),
        in_specs=[
            tok(D),
            pl.BlockSpec((None, 1, D), lambda i: (layer, 0, 0)),
            pl.BlockSpec((None, D, din), lambda i: (j, 0, 0)),
            pl.BlockSpec((None, 1, ql), lambda i: (j, 0, 0)),
            pl.BlockSpec((None, ql, n_heads * nope), lambda i: (j, 0, 0)),
            pl.BlockSpec((None, ql, n_heads * rope), lambda i: (j, 0, 0)),
            pl.BlockSpec((None, 1, kvl), lambda i: (j, 0, 0)),
            pl.BlockSpec((None, n_heads, nope, kvl), lambda i: (j, 0, 0, 0)),
            pos(n_heads * rope), pos(n_heads * rope), pos(rope), pos(rope),
        ],
        out_specs=[
            pl.BlockSpec((n_heads, tm, kvl), lambda i: (0, i, 0)),
            pl.BlockSpec((n_heads, tm, rope), lambda i: (0, i, 0)),
            tok(kvl), tok(rope), tok(kvl), tok(rope),
        ],
        out_shape=[
            jax.ShapeDtypeStruct((n_heads, T, kvl), BF16),
            jax.ShapeDtypeStruct((n_heads, T, rope), BF16),
            jax.ShapeDtypeStruct((T, kvl), F32),
            jax.ShapeDtypeStruct((T, rope), F32),
            jax.ShapeDtypeStruct((T, kvl), BF16),
            jax.ShapeDtypeStruct((T, rope), BF16),
        ],
        compiler_params=_params("parallel"),
        name="mla_proj",
    )(x, pre, win, qg, wuqn, wuqr, kvg, wukt, cosq, sinq, cosk, sink)


def _mla_out(o_heads, x, post, wuv_ref, wo_ref):
    cat = jnp.concatenate(
        [_dot(o.astype(BF16), wuv_ref[hd]).astype(BF16) for hd, o in enumerate(o_heads)], axis=-1)
    return x + _rmsnorm(_dot(cat, wo_ref[...]), post)


def _attn_prompt_body(sm_scale, ql_ref, qr_ref, kc_ref, kr_ref, x_ref, post_ref, wuv_ref, wo_ref,
                      o_ref, m_scr, l_scr, acc_scr):
    qi = pl.program_id(1)
    n_heads, tq, kvl = ql_ref.shape
    rows = n_heads * tq
    ql = ql_ref[...].reshape(rows, kvl)
    qr = qr_ref[...].reshape(rows, qr_ref.shape[-1])
    m_scr[...] = jnp.full(m_scr.shape, -jnp.inf, F32)
    l_scr[...] = jnp.zeros(l_scr.shape, F32)
    acc_scr[...] = jnp.zeros(acc_scr.shape, F32)

    def step(j, masked):
        start = pl.multiple_of(j * tq, tq)
        kc = kc_ref[pl.ds(start, tq), :]
        kr = kr_ref[pl.ds(start, tq), :]
        s = (_dot_nt(ql, kc) + _dot_nt(qr, kr)) * sm_scale
        if masked:
            q_pos = lax.broadcasted_iota(jnp.int32, s.shape, 0) % tq
            k_pos = lax.broadcasted_iota(jnp.int32, s.shape, 1)
            s = jnp.where(k_pos <= q_pos, s, -jnp.inf)
        m_prev = m_scr[...]
        m_new = jnp.maximum(m_prev, jnp.max(s, axis=-1, keepdims=True))
        alpha = jnp.exp(m_prev - m_new)
        p = jnp.exp(s - m_new)
        l_scr[...] = alpha * l_scr[...] + jnp.sum(p, axis=-1, keepdims=True)
        acc_scr[...] = alpha * acc_scr[...] + _dot(p.astype(BF16), kc)
        m_scr[...] = m_new

    def full_block(j, carry):
        step(j, False)
        return carry

    lax.fori_loop(0, qi, full_block, 0)
    step(qi, True)

    o = acc_scr[...] / l_scr[...]
    o_heads = [o[hd * tq:(hd + 1) * tq, :] for hd in range(n_heads)]
    o_ref[...] = _mla_out(o_heads, x_ref[...], post_ref[...], wuv_ref, wo_ref)


def _attn_prompt(ql, qr, kc, kr, x, j, layer, batch, post, wuv, wo, sm_scale):
    n_heads, T, kvl = ql.shape
    rope = qr.shape[-1]
    D = x.shape[-1]
    S = T // batch
    tq = ATTN_TQ
    nq = S // tq
    vd = wuv.shape[-1]
    rows = n_heads * tq
    return pl.pallas_call(
        functools.partial(_attn_prompt_body, sm_scale),
        grid=(batch, nq),
        in_specs=[
            pl.BlockSpec((n_heads, tq, kvl), lambda b, q: (0, b * nq + q, 0)),
            pl.BlockSpec((n_heads, tq, rope), lambda b, q: (0, b * nq + q, 0)),
            pl.BlockSpec((S, kvl), lambda b, q: (b, 0)),
            pl.BlockSpec((S, rope), lambda b, q: (b, 0)),
            pl.BlockSpec((tq, D), lambda b, q: (b * nq + q, 0)),
            pl.BlockSpec((None, 1, D), lambda b, q: (layer, 0, 0)),
            pl.BlockSpec((None, n_heads, kvl, vd), lambda b, q: (j, 0, 0, 0)),
            pl.BlockSpec((None, n_heads * vd, D), lambda b, q: (j, 0, 0)),
        ],
        out_specs=pl.BlockSpec((tq, D), lambda b, q: (b * nq + q, 0)),
        out_shape=jax.ShapeDtypeStruct((T, D), F32),
        scratch_shapes=[
            pltpu.VMEM((rows, 1), F32),
            pltpu.VMEM((rows, 1), F32),
            pltpu.VMEM((rows, kvl), F32),
        ],
        compiler_params=_params("parallel", "arbitrary"),
        name="attn_prompt",
    )(ql, qr, kc, kr, x, post, wuv, wo)


def _attn_sample_body(sm_scale, n_pages, pt_ref, ql_ref, qr_ref, cn_ref, rn_ref, *refs):
    kv_refs = refs[:n_pages]
    kr_refs = refs[n_pages:2 * n_pages]
    o_ref, m_scr, l_scr, acc_scr = refs[2 * n_pages:]
    g = pl.program_id(1)

    @pl.when(g == 0)
    def _():
        m_scr[...] = jnp.full(m_scr.shape, -jnp.inf, F32)
        l_scr[...] = jnp.zeros(l_scr.shape, F32)
        acc_scr[...] = jnp.zeros(acc_scr.shape, F32)

    ql = ql_ref[...]
    qr = qr_ref[...]
    kc = jnp.concatenate([r[...].astype(BF16) for r in kv_refs], axis=0)
    kr = jnp.concatenate([r[...].astype(BF16) for r in kr_refs], axis=0)
    s = (_dot_nt(ql, kc) + _dot_nt(qr, kr)) * sm_scale
    m_prev = m_scr[...]
    m_new = jnp.maximum(m_prev, jnp.max(s, axis=-1, keepdims=True))
    alpha = jnp.exp(m_prev - m_new)
    p = jnp.exp(s - m_new)
    l_scr[...] = alpha * l_scr[...] + jnp.sum(p, axis=-1, keepdims=True)
    acc_scr[...] = alpha * acc_scr[...] + _dot(p.astype(BF16), kc)
    m_scr[...] = m_new

    @pl.when(g == pl.num_programs(1) - 1)
    def _():
        tt = cn_ref.shape[0]
        qlf = ql.astype(F32)
        qrf = qr.astype(F32)
        q_t = lax.broadcasted_iota(jnp.int32, (ql.shape[0], 1), 0) % tt
        m = m_scr[...]
        l = l_scr[...]
        acc = acc_scr[...]
        for t in range(tt):
            c_t = cn_ref[t:t + 1, :].astype(BF16).astype(F32)
            r_t = rn_ref[t:t + 1, :].astype(BF16).astype(F32)
            s_t = (jnp.sum(qlf * c_t, axis=-1, keepdims=True)
                   + jnp.sum(qrf * r_t, axis=-1, keepdims=True)) * sm_scale
            valid = q_t >= t
            m_new = jnp.where(valid, jnp.maximum(m, s_t), m)
            alpha = jnp.exp(m - m_new)
            p_t = jnp.where(valid, jnp.exp(s_t - m_new), 0.0)
            l = alpha * l + p_t
            acc = alpha * acc + p_t * c_t
            m = m_new
        o_ref[...] = acc / l


def _attn_sample(ql, qr, c_new, r_new, cache_kv, cache_kr, page_table, j, sm_scale):
    B, rows, kvl = ql.shape
    rope = qr.shape[-1]
    tt = c_new.shape[1]
    page = cache_kv.shape[2]
    n_pages_total = page_table.shape[1]
    G = PAGES_PER_STEP
    pt = page_table.reshape(-1)

    def page_map(k, b, g, pt_ref):
        return (j, pt_ref[b * n_pages_total + g * G + k], 0, 0)

    per_b = lambda w: pl.BlockSpec((None, rows, w), lambda b, g, pt_ref: (b, 0, 0))
    new_b = lambda w: pl.BlockSpec((None, tt, w), lambda b, g, pt_ref: (b, 0, 0))
    kv_specs = [pl.BlockSpec((None, None, page, kvl), functools.partial(page_map, k)) for k in range(G)]
    kr_specs = [pl.BlockSpec((None, None, page, rope), functools.partial(page_map, k)) for k in range(G)]
    return pl.pallas_call(
        functools.partial(_attn_sample_body, sm_scale, G),
        grid_spec=pltpu.PrefetchScalarGridSpec(
            num_scalar_prefetch=1,
            grid=(B, n_pages_total // G),
            in_specs=[per_b(kvl), per_b(rope), new_b(kvl), new_b(rope)] + kv_specs + kr_specs,
            out_specs=pl.BlockSpec((None, rows, kvl), lambda b, g, pt_ref: (b, 0, 0)),
            scratch_shapes=[
                pltpu.VMEM((rows, 1), F32),
                pltpu.VMEM((rows, 1), F32),
                pltpu.VMEM((rows, kvl), F32),
            ],
        ),
        out_shape=jax.ShapeDtypeStruct((B, rows, kvl), F32),
        compiler_params=_params("parallel", "arbitrary"),
        name="attn_sample",
    )(pt, ql, qr, c_new, r_new, *([cache_kv] * G), *([cache_kr] * G))


def _mla_out_sample_body(oh_ref, x_ref, post_ref, wuv_ref, wo_ref, o_ref):
    o_heads = [oh_ref[hd] for hd in range(oh_ref.shape[0])]
    o_ref[...] = _mla_out(o_heads, x_ref[...], post_ref[...], wuv_ref, wo_ref)


def _mla_out_sample(oh, x, j, layer, post, wuv, wo):
    n_heads, T, kvl = oh.shape
    D = x.shape[-1]
    vd = wuv.shape[-1]
    return pl.pallas_call(
        _mla_out_sample_body,
        grid=(1,),
        in_specs=[
            pl.BlockSpec((n_heads, T, kvl), lambda g: (0, 0, 0)),
            pl.BlockSpec((T, D), lambda g: (0, 0)),
            pl.BlockSpec((None, 1, D), lambda g: (layer, 0, 0)),
            pl.BlockSpec((None, n_heads, kvl, vd), lambda g: (j, 0, 0, 0)),
            pl.BlockSpec((None, n_heads * vd, D), lambda g: (j, 0, 0)),
        ],
        out_specs=pl.BlockSpec((T, D), lambda g: (0, 0)),
        out_shape=jax.ShapeDtypeStruct((T, D), F32),
        compiler_params=_params("arbitrary"),
        name="mla_out_sample",
    )(oh, x, post, wuv, wo)


def _rope_tables(pos, rope, n_heads):
    half = rope // 2
    inv_freq = 1.0 / (ROPE_THETA ** (jnp.arange(half, dtype=F32) * (2.0 / rope)))
    ang = pos.astype(F32)[:, None] * inv_freq[None, :]
    cos, sin = jnp.cos(ang), jnp.sin(ang)
    cos_k = jnp.concatenate([cos, cos], axis=-1)
    sin_k = jnp.concatenate([-sin, sin], axis=-1)
    return jnp.tile(cos_k, (1, n_heads)), jnp.tile(sin_k, (1, n_heads)), cos_k, sin_k


def kernel(x_prompt, x_sample, state_conv_a, state_conv_b, cache_kv_latent, cache_k_rope, page_table,
           mix_pre_gain, mix_post_gain, ffn_pre_gain, ffn_post_gain, w_ffn_gate, w_ffn_up, w_ffn_down,
           w_in_conv, conv_a_w, conv_a_b, ln_a_gain, ln_a_bias, conv_b_w, w_out_conv,
           w_in_mla, q_norm_gain, w_uq, kv_norm_gain, w_uk, w_uv, w_o_mla):
    Bp, Sp, D = x_prompt.shape
    Bs, Ss, _ = x_sample.shape
    depth = mix_pre_gain.shape[0]
    n_mla = w_in_mla.shape[0]
    kvl, n_heads, nope = w_uk.shape[1:]
    ql = q_norm_gain.shape[-1]
    rope = w_in_mla.shape[-1] - ql - kvl
    sm_scale = float(nope + rope) ** -0.5
    past_len = page_table.shape[1] * cache_kv_latent.shape[2]

    row3 = lambda a: a.reshape(a.shape[0], 1, a.shape[1])
    mix_pre, mix_post = row3(mix_pre_gain), row3(mix_post_gain)
    ffn_pre, ffn_post = row3(ffn_pre_gain), row3(ffn_post_gain)
    wg, wu, wd = w_ffn_gate.astype(BF16), w_ffn_up.astype(BF16), w_ffn_down.astype(BF16)
    win_c, wout_c = w_in_conv.astype(BF16), w_out_conv.astype(BF16)
    ba, lng, lnb = row3(conv_a_b), row3(ln_a_gain), row3(ln_a_bias)
    win_m = w_in_mla.astype(BF16)
    qg, kvg = row3(q_norm_gain), row3(kv_norm_gain)
    wuq4 = w_uq.reshape(n_mla, ql, n_heads, nope + rope)
    wuqn = wuq4[..., :nope].reshape(n_mla, ql, n_heads * nope).astype(BF16)
    wuqr = wuq4[..., nope:].reshape(n_mla, ql, n_heads * rope).astype(BF16)
    wukt = jnp.transpose(w_uk, (0, 2, 3, 1)).astype(BF16)
    wuv = jnp.transpose(w_uv, (0, 2, 1, 3)).astype(BF16)
    wo = w_o_mla.astype(BF16)

    tab_p = _rope_tables(jnp.arange(Sp), rope, n_heads)
    tab_s = _rope_tables(past_len + jnp.repeat(jnp.arange(Ss), Bs), rope, n_heads)

    yp = x_prompt.reshape(Bp * Sp, D)
    ys = jnp.transpose(x_sample, (1, 0, 2))
    sa_tm = jnp.transpose(state_conv_a, (0, 2, 1, 3))
    sb_tm = jnp.transpose(state_conv_b, (0, 2, 1, 3))

    ca_p, ca_s, cb_p, cb_s = [], [], [], []
    kv_p, kv_s, kr_p, kr_s = [], [], [], []
    for layer in range(depth):
        if layer % 2 == 0:
            i = layer // 2
            prm = (mix_pre, mix_post, win_c, conv_a_w, ba, lng, lnb, conv_b_w, wout_c)
            yp, na_p, nb_p = _conv_prompt(yp, i, layer, Bp, *prm)
            ys, na_s, nb_s = _conv_sample(ys, sa_tm, sb_tm, i, layer, *prm)
            ca_p.append(na_p); cb_p.append(nb_p); ca_s.append(na_s); cb_s.append(nb_s)
        else:
            j = layer // 2
            prm = (mix_pre, win_m, qg, wuqn, wuqr, kvg, wukt)
            qlp, qrp, c_p, r_p, cb16, rb16 = _mla_proj(yp, j, layer, *prm, *tab_p)
            yp = _attn_prompt(qlp, qrp, cb16, rb16, yp, j, layer, Bp, mix_post, wuv, wo, sm_scale)
            ys2 = ys.reshape(Ss * Bs, D)
            qls, qrs, c_s, r_s, _, _ = _mla_proj(ys2, j, layer, *prm, *tab_s)
            to_b = lambda a: jnp.transpose(a.reshape(n_heads, Ss, Bs, a.shape[-1]), (2, 0, 1, 3)).reshape(
                Bs, n_heads * Ss, a.shape[-1])
            c_sb = jnp.transpose(c_s.reshape(Ss, Bs, kvl), (1, 0, 2))
            r_sb = jnp.transpose(r_s.reshape(Ss, Bs, rope), (1, 0, 2))
            o_lat = _attn_sample(to_b(qls), to_b(qrs), c_sb, r_sb, cache_kv_latent, cache_k_rope,
                                 page_table, j, sm_scale)
            oh = jnp.transpose(o_lat.reshape(Bs, n_heads, Ss, kvl), (1, 2, 0, 3)).reshape(n_heads, Ss * Bs, kvl)
            ys = _mla_out_sample(oh, ys2, j, layer, mix_post, wuv, wo).reshape(Ss, Bs, D)
            kv_p.append(c_p.reshape(Bp, Sp, kvl)); kr_p.append(r_p.reshape(Bp, Sp, rope))
            kv_s.append(c_sb); kr_s.append(r_sb)
        yp = _ffn(yp, layer, ffn_pre, ffn_post, wg, wu, wd)
        ys = _ffn(ys.reshape(Ss * Bs, D), layer, ffn_pre, ffn_post, wg, wu, wd).reshape(Ss, Bs, D)

    from_tm = lambda xs: jnp.transpose(jnp.stack(xs), (0, 2, 1, 3))
    return (yp.reshape(Bp, Sp, D), jnp.transpose(ys, (1, 0, 2)),
            jnp.stack(ca_p), from_tm(ca_s), jnp.stack(cb_p), from_tm(cb_s),
            jnp.stack(kv_p), jnp.stack(kv_s), jnp.stack(kr_p), jnp.stack(kr_s))
```

```python
import functools

import jax
import jax.numpy as jnp
from jax import lax
from jax.experimental import pallas as pl
from jax.experimental.pallas import tpu as pltpu

F32 = jnp.float32
BF16 = jnp.bfloat16

ROPE_THETA = 10000.0
RMS_EPS = 1e-6
LN_EPS = 1e-5
LOG2_E = 1.4426950408889634

V7X_VMEM_LIMIT_BYTES = 56 * 1024 * 1024
SUBLANES = 8
LANES = 128

FFN_TM = 1024
FFN_TF = 256
CONV_TS = 512
CONV_RC = 64
CONV_HIST = 32
CONV_SAMPLE_NB = 64
PROJ_TM = 512
ATTN_TQ = 256


def _params(*sem):
    return pltpu.CompilerParams(dimension_semantics=sem, vmem_limit_bytes=V7X_VMEM_LIMIT_BYTES)


def _rmsnorm(x, g):
    return x * lax.rsqrt(jnp.mean(x * x, axis=-1, keepdims=True) + RMS_EPS) * g


def _layernorm(x, g, b):
    mu = jnp.mean(x, axis=-1, keepdims=True)
    xc = x - mu
    var = jnp.mean(xc * xc, axis=-1, keepdims=True)
    return xc * lax.rsqrt(var + LN_EPS) * g + b


def _silu(x):
    return x * jax.nn.sigmoid(x)


def _dot(a, b):
    return jnp.dot(a, b, preferred_element_type=F32)


def _dot_nt(a, b):
    return lax.dot_general(a, b, (((1,), (1,)), ((), ())), preferred_element_type=F32)


def _ffn_body(x_ref, pre_ref, post_ref, wg_ref, wu_ref, wd_ref, o_ref, h_scr, act_scr):
    h_scr[...] = _rmsnorm(x_ref[...], pre_ref[...]).astype(BF16)
    for c in range(wg_ref.shape[-1] // FFN_TF):
        cols = slice(c * FFN_TF, (c + 1) * FFN_TF)
        gate = _dot(h_scr[...], wg_ref[:, cols])
        up = _dot(h_scr[...], wu_ref[:, cols])
        act_scr[:, cols] = (_silu(gate) * up).astype(BF16)
    o_ref[...] = x_ref[...] + _rmsnorm(_dot(act_scr[...], wd_ref[...]), post_ref[...])


def _ffn(x, layer, pre, post, wg, wu, wd):
    T, D = x.shape
    F = wg.shape[-1]
    tm = min(FFN_TM, T)
    gain_spec = pl.BlockSpec((None, 1, D), lambda i: (layer, 0, 0))
    resident = dict(pipeline_mode=pl.Buffered(1))
    return pl.pallas_call(
        _ffn_body,
        grid=(T // tm,),
        in_specs=[
            pl.BlockSpec((tm, D), lambda i: (i, 0)),
            gain_spec, gain_spec,
            pl.BlockSpec((None, D, F), lambda i: (layer, 0, 0), **resident),
            pl.BlockSpec((None, D, F), lambda i: (layer, 0, 0), **resident),
            pl.BlockSpec((None, F, D), lambda i: (layer, 0, 0), **resident),
        ],
        out_specs=pl.BlockSpec((tm, D), lambda i: (i, 0)),
        out_shape=jax.ShapeDtypeStruct((T, D), F32),
        scratch_shapes=[pltpu.VMEM((tm, D), BF16), pltpu.VMEM((tm, F), BF16)],
        compiler_params=_params("parallel"),
        name="ffn",
    )(x, pre, post, wg, wu, wd)


def _conv_prompt_body(x_ref, pre_ref, post_ref, win_ref, wa_ref, ba_ref, lng_ref, lnb_ref, wb_ref, wout_ref,
                      o_ref, sa_ref, sb_ref, u_scr, apad_scr, shift_scr, zpad_scr, cat_scr):
    t = pl.program_id(1)
    ts = x_ref.shape[0]
    ca = apad_scr.shape[1]
    wa_n = wa_ref.shape[0]
    wb_n = wb_ref.shape[0]

    @pl.when(t == 0)
    def _():
        apad_scr[0:CONV_HIST, :] = jnp.zeros((CONV_HIST, ca), F32)
        zpad_scr[0:SUBLANES, :] = jnp.zeros((SUBLANES, ca), F32)

    x = x_ref[...]
    h = _rmsnorm(x, pre_ref[...]).astype(BF16)
    u_scr[...] = _dot(h, win_ref[...])
    apad_scr[CONV_HIST:CONV_HIST + ts, :] = u_scr[:, 0:ca] * jax.nn.sigmoid(u_scr[:, ca:2 * ca])
    zpad_scr[SUBLANES:SUBLANES + ts, :] = u_scr[:, 3 * ca:4 * ca] * u_scr[:, 4 * ca:5 * ca]

    a0 = CONV_HIST - (wa_n - 1)
    z0 = SUBLANES - (wb_n - 1)
    n_shift_rows = shift_scr.shape[1]
    for r in range(1, SUBLANES):
        shift_scr[r - 1] = apad_scr[r:r + n_shift_rows, :]

    def tap_rows(r0, k):
        off = a0 + k
        base, r = off - off % SUBLANES, off % SUBLANES
        if r == 0:
            return apad_scr[r0 + base:r0 + base + CONV_RC, :]
        return shift_scr[r - 1, r0 + base:r0 + base + CONV_RC, :]

    for c in range(ts // CONV_RC):
        r0 = c * CONV_RC
        acc = jnp.broadcast_to(ba_ref[...], (CONV_RC, ca))
        for k in range(wa_n):
            acc = acc + wa_ref[k:k + 1, :] * tap_rows(r0, k)
        a = _silu(_layernorm(acc, lng_ref[...], lnb_ref[...]))
        cat_scr[r0:r0 + CONV_RC, 0:ca] = a.astype(BF16)
        cb = wb_ref[0:1, :] * zpad_scr[r0 + z0:r0 + z0 + CONV_RC, :]
        for k in range(1, wb_n):
            cb = cb + wb_ref[k:k + 1, :] * zpad_scr[r0 + z0 + k:r0 + z0 + k + CONV_RC, :]
        b = u_scr[r0:r0 + CONV_RC, 2 * ca:3 * ca] * cb
        cat_scr[r0:r0 + CONV_RC, ca:2 * ca] = b.astype(BF16)

    out = _dot(cat_scr[...], wout_ref[...])
    o_ref[...] = x + _rmsnorm(out, post_ref[...])

    @pl.when(t == pl.num_programs(1) - 1)
    def _():
        sa_ref[...] = apad_scr[CONV_HIST + ts - (wa_n - 1):CONV_HIST + ts, :]
        sb_ref[...] = zpad_scr[SUBLANES + ts - (wb_n - 1):SUBLANES + ts, :]

    apad_scr[0:CONV_HIST, :] = apad_scr[ts:ts + CONV_HIST, :]
    zpad_scr[0:SUBLANES, :] = zpad_scr[ts:ts + SUBLANES, :]


def _conv_prompt(x, i, layer, batch, pre, post, win, wa, ba, lng, lnb, wb, wout):
    T, D = x.shape
    S = T // batch
    ts = CONV_TS
    nt = S // ts
    ca = wa.shape[-1]
    wa_n, wb_n = wa.shape[1], wb.shape[1]
    assert wa_n - 1 <= CONV_HIST and wb_n - 1 <= SUBLANES
    gain_spec = pl.BlockSpec((None, 1, D), lambda b, t: (layer, 0, 0))
    vec_spec = pl.BlockSpec((None, 1, ca), lambda b, t: (i, 0, 0))
    return pl.pallas_call(
        _conv_prompt_body,
        grid=(batch, nt),
        in_specs=[
            pl.BlockSpec((ts, D), lambda b, t: (b * nt + t, 0)),
            gain_spec, gain_spec,
            pl.BlockSpec((None, D, 5 * ca), lambda b, t: (i, 0, 0)),
            pl.BlockSpec((None, wa_n, ca), lambda b, t: (i, 0, 0)),
            vec_spec, vec_spec, vec_spec,
            pl.BlockSpec((None, wb_n, ca), lambda b, t: (i, 0, 0)),
            pl.BlockSpec((None, 2 * ca, D), lambda b, t: (i, 0, 0)),
        ],
        out_specs=[
            pl.BlockSpec((ts, D), lambda b, t: (b * nt + t, 0)),
            pl.BlockSpec((None, wa_n - 1, ca), lambda b, t: (b, 0, 0)),
            pl.BlockSpec((None, wb_n - 1, ca), lambda b, t: (b, 0, 0)),
        ],
        out_shape=[
            jax.ShapeDtypeStruct((T, D), F32),
            jax.ShapeDtypeStruct((batch, wa_n - 1, ca), F32),
            jax.ShapeDtypeStruct((batch, wb_n - 1, ca), F32),
        ],
        scratch_shapes=[
            pltpu.VMEM((ts, 5 * ca), F32),
            pltpu.VMEM((CONV_HIST + ts, ca), F32),
            pltpu.VMEM((SUBLANES - 1, CONV_HIST + ts - SUBLANES, ca), F32),
            pltpu.VMEM((SUBLANES + ts, ca), F32),
            pltpu.VMEM((ts, 2 * ca), BF16),
        ],
        compiler_params=_params("parallel", "arbitrary"),
        name="conv_prompt",
    )(x, pre, post, win, wa, ba, lng, lnb, wb, wout)


def _conv_sample_body(x_ref, sa_ref, sb_ref, pre_ref, post_ref, win_ref, wa_ref, ba_ref, lng_ref, lnb_ref,
                      wb_ref, wout_ref, o_ref, nsa_ref, nsb_ref, u_scr, glu_scr, z_scr, cat_scr):
    tt, nb, d = x_ref.shape
    ca = glu_scr.shape[-1]
    wa_n = wa_ref.shape[0]
    wb_n = wb_ref.shape[0]
    x = x_ref[...].reshape(tt * nb, d)
    h = _rmsnorm(x, pre_ref[...]).astype(BF16)
    u_scr[...] = _dot(h, win_ref[...])
    for t in range(tt):
        rows = slice(t * nb, (t + 1) * nb)
        glu_scr[t] = u_scr[rows, 0:ca] * jax.nn.sigmoid(u_scr[rows, ca:2 * ca])
        z_scr[t] = u_scr[rows, 3 * ca:4 * ca] * u_scr[rows, 4 * ca:5 * ca]

    def apad(j, rs):
        return sa_ref[j, rs, :] if j < wa_n - 1 else glu_scr[j - (wa_n - 1), rs, :]

    def zpad(j, rs):
        return sb_ref[j, rs, :] if j < wb_n - 1 else z_scr[j - (wb_n - 1), rs, :]

    for t in range(tt):
        for r in range(nb // CONV_RC):
            rs = slice(r * CONV_RC, (r + 1) * CONV_RC)
            orows = slice(t * nb + r * CONV_RC, t * nb + (r + 1) * CONV_RC)
            acc = jnp.broadcast_to(ba_ref[...], (CONV_RC, ca))
            for k in range(wa_n):
                acc = acc + wa_ref[k:k + 1, :] * apad(t + k, rs)
            a = _silu(_layernorm(acc, lng_ref[...], lnb_ref[...]))
            cat_scr[orows, 0:ca] = a.astype(BF16)
            cb = wb_ref[0:1, :] * zpad(t, rs)
            for k in range(1, wb_n):
                cb = cb + wb_ref[k:k + 1, :] * zpad(t + k, rs)
            cat_scr[orows, ca:2 * ca] = (u_scr[orows, 2 * ca:3 * ca] * cb).astype(BF16)

    out = _dot(cat_scr[...], wout_ref[...])
    o_ref[...] = (x + _rmsnorm(out, post_ref[...])).reshape(tt, nb, d)
    full = slice(0, nb)
    for j in range(wa_n - 1):
        nsa_ref[j] = apad(j + tt, full)
    for j in range(wb_n - 1):
        nsb_ref[j] = zpad(j + tt, full)


def _conv_sample(x, sa, sb, i, layer, pre, post, win, wa, ba, lng, lnb, wb, wout):
    tt, B, D = x.shape
    nb = CONV_SAMPLE_NB
    ca = wa.shape[-1]
    wa_n, wb_n = wa.shape[1], wb.shape[1]
    gain_spec = pl.BlockSpec((None, 1, D), lambda g: (layer, 0, 0))
    vec_spec = pl.BlockSpec((None, 1, ca), lambda g: (i, 0, 0))
    return pl.pallas_call(
        _conv_sample_body,
        grid=(B // nb,),
        in_specs=[
            pl.BlockSpec((tt, nb, D), lambda g: (0, g, 0)),
            pl.BlockSpec((None, wa_n - 1, nb, ca), lambda g: (i, 0, g, 0)),
            pl.BlockSpec((None, wb_n - 1, nb, ca), lambda g: (i, 0, g, 0)),
            gain_spec, gain_spec,
            pl.BlockSpec((None, D, 5 * ca), lambda g: (i, 0, 0)),
            pl.BlockSpec((None, wa_n, ca), lambda g: (i, 0, 0)),
            vec_spec, vec_spec, vec_spec,
            pl.BlockSpec((None, wb_n, ca), lambda g: (i, 0, 0)),
            pl.BlockSpec((None, 2 * ca, D), lambda g: (i, 0, 0)),
        ],
        out_specs=[
            pl.BlockSpec((tt, nb, D), lambda g: (0, g, 0)),
            pl.BlockSpec((wa_n - 1, nb, ca), lambda g: (0, g, 0)),
            pl.BlockSpec((wb_n - 1, nb, ca), lambda g: (0, g, 0)),
        ],
        out_shape=[
            jax.ShapeDtypeStruct((tt, B, D), F32),
            jax.ShapeDtypeStruct((wa_n - 1, B, ca), F32),
            jax.ShapeDtypeStruct((wb_n - 1, B, ca), F32),
        ],
        scratch_shapes=[
            pltpu.VMEM((tt * nb, 5 * ca), F32),
            pltpu.VMEM((tt, nb, ca), F32),
            pltpu.VMEM((tt, nb, ca), F32),
            pltpu.VMEM((tt * nb, 2 * ca), BF16),
        ],
        compiler_params=_params("parallel"),
        name="conv_sample",
    )(x, sa, sb, pre, post, win, wa, ba, lng, lnb, wb, wout)


def _mla_proj_body(x_ref, pre_ref, win_ref, qg_ref, wuqn_ref, wuqr_ref, kvg_ref, wukt_ref,
                   cosq_ref, sinq_ref, cosk_ref, sink_ref,
                   qlat_ref, qrope_ref, ckv_ref, kr_ref, ckvb_ref, krb_ref):
    n_heads, _, kvl = qlat_ref.shape
    rope = qrope_ref.shape[-1]
    half = rope // 2
    ql = qg_ref.shape[-1]
    nope = wuqn_ref.shape[-1] // n_heads

    h = _rmsnorm(x_ref[...], pre_ref[...]).astype(BF16)
    u = _dot(h, win_ref[...])
    cq, ckv, kr = u[:, :ql], u[:, ql:ql + kvl], u[:, ql + kvl:]
    qn = _rmsnorm(cq, qg_ref[...]).astype(BF16)
    q_nope = _dot(qn, wuqn_ref[...])
    q_r = _dot(qn, wuqr_ref[...])
    width = n_heads * rope
    lane = lax.broadcasted_iota(jnp.int32, q_r.shape, 1)
    swapped = jnp.where(lane % rope < half,
                        pltpu.roll(q_r, width - half, 1), pltpu.roll(q_r, half, 1))
    q_rot = q_r * cosq_ref[...] + swapped * sinq_ref[...]
    for hd in range(n_heads):
        q_h = q_nope[:, hd * nope:(hd + 1) * nope].astype(BF16)
        qlat_ref[hd] = _dot(q_h, wukt_ref[hd]).astype(BF16)
        qrope_ref[hd] = q_rot[:, hd * rope:(hd + 1) * rope].astype(BF16)

    c = _rmsnorm(ckv, kvg_ref[...])
    ckv_ref[...] = c
    ckvb_ref[...] = c.astype(BF16)
    kr_swapped = jnp.concatenate([kr[:, half:], kr[:, :half]], axis=-1)
    kr_rot = kr * cosk_ref[...] + kr_swapped * sink_ref[...]
    kr_ref[...] = kr_rot
    krb_ref[...] = kr_rot.astype(BF16)


def _mla_proj(x, j, layer, pre, win, qg, wuqn, wuqr, kvg, wukt, cosq, sinq, cosk, sink):
    T, D = x.shape
    tm = min(PROJ_TM, T)
    n_pos = cosq.shape[0] // tm
    n_heads, nope, kvl = wukt.shape[1:]
    ql = qg.shape[-1]
    rope = cosk.shape[-1]
    din = win.shape[-1]
    tok = lambda w: pl.BlockSpec((tm, w), lambda i: (i, 0))
    pos = lambda w: pl.BlockSpec((tm, w), lambda i: (i % n_pos, 0))
    return pl.pallas_call(
        _mla_proj_body,
        grid=(T // tm,),
        in_specs=[
            tok(D),
            pl.BlockSpec((None, 1, D), lambda i: (layer, 0, 0)),
            pl.BlockSpec((None, D, din), lambda i: (j, 0, 0)),
            pl.BlockSpec((None, 1, ql), lambda i: (j, 0, 0)),
            pl.BlockSpec((None, ql, n_heads * nope), lambda i: (j, 0, 0)),
            pl.BlockSpec((None, ql, n_heads * rope), lambda i: (j, 0, 0)),
            pl.BlockSpec((None, 1, kvl), lambda i: (j, 0, 0)),
            pl.BlockSpec((None, n_heads, nope, kvl), lambda i: (j, 0, 0, 0)),
            pos(n_heads * rope), pos(n_heads * rope), pos(rope), pos(rope),
        ],
        out_specs=[
            pl.BlockSpec((n_heads, tm, kvl), lambda i: (0, i, 0)),
            pl.BlockSpec((n_heads, tm, rope), lambda i: (0, i, 0)),
            tok(kvl), tok(rope), tok(kvl), tok(rope),
        ],
        out_shape=[
            jax.ShapeDtypeStruct((n_heads, T, kvl), BF16),
            jax.ShapeDtypeStruct((n_heads, T, rope), BF16),
            jax.ShapeDtypeStruct((T, kvl), F32),
            jax.ShapeDtypeStruct((T, rope), F32),
            jax.ShapeDtypeStruct((T, kvl), BF16),
            jax.ShapeDtypeStruct((T, rope), BF16),
        ],
        compiler_params=_params("parallel"),
        name="mla_proj",
    )(x, pre, win, qg, wuqn, wuqr, kvg, wukt, cosq, sinq, cosk, sink)


def _mla_out(o_heads, x, post, wuv_ref, wo_ref):
    cat = jnp.concatenate(
        [_dot(o.astype(BF16), wuv_ref[hd]).astype(BF16) for hd, o in enumerate(o_heads)], axis=-1)
    return x + _rmsnorm(_dot(cat, wo_ref[...]), post)


def _attn_prompt_body(sm_scale, ql_ref, qr_ref, kc_ref, kr_ref, x_ref, post_ref, wuv_ref, wo_ref,
                      o_ref, m_scr, l_scr, acc_scr):
    qi = pl.program_id(1)
    n_heads, tq, kvl = ql_ref.shape
    rows = n_heads * tq
    lanes = m_scr.shape[-1]
    c_exp = sm_scale * LOG2_E
    ql = ql_ref[...].reshape(rows, kvl)
    qr = qr_ref[...].reshape(rows, qr_ref.shape[-1])

    def lane_groups(a):
        return [a[:, i * lanes:(i + 1) * lanes] for i in range(tq // lanes)]

    def block(start, diagonal):
        kc = kc_ref[pl.ds(start, tq), :]
        s = _dot_nt(ql, kc) + _dot_nt(qr, kr_ref[pl.ds(start, tq), :])
        if diagonal:
            q_pos = lax.broadcasted_iota(jnp.int32, s.shape, 0) % tq
            k_pos = lax.broadcasted_iota(jnp.int32, s.shape, 1)
            s = jnp.where(k_pos <= q_pos, s, -jnp.inf)
        s_parts = lane_groups(s)
        m_cur = jnp.max(functools.reduce(jnp.maximum, s_parts), axis=-1, keepdims=True)
        if diagonal:
            m_new = jnp.broadcast_to(m_cur, (rows, lanes))
        else:
            m_prev = m_scr[...]
            m_new = jnp.maximum(m_prev, m_cur)
            alpha = jnp.exp2((m_prev - m_new) * c_exp)
        p_parts = [jnp.exp2((sp - m_new) * c_exp) for sp in s_parts]
        p = jnp.concatenate([pp.astype(BF16) for pp in p_parts], axis=-1)
        pv = _dot(p, kc)
        if diagonal:
            l_scr[...] = functools.reduce(jnp.add, p_parts)
            acc_scr[...] = pv
        else:
            l_scr[...] = alpha * l_scr[...] + functools.reduce(jnp.add, p_parts)
            acc_scr[...] = jnp.concatenate([alpha] * (kvl // lanes), axis=-1) * acc_scr[...] + pv
        m_scr[...] = m_new

    block(pl.multiple_of(qi * tq, tq), True)

    def full_block(j, carry):
        block(pl.multiple_of(j * tq, tq), False)
        return carry

    lax.fori_loop(0, qi, full_block, 0)

    o_heads = []
    for hd in range(n_heads):
        rs = slice(hd * tq, (hd + 1) * tq)
        o_heads.append(acc_scr[rs, :] / jnp.sum(l_scr[rs, :], axis=-1, keepdims=True))
    o_ref[...] = _mla_out(o_heads, x_ref[...], post_ref[...], wuv_ref, wo_ref)


def _attn_prompt(ql, qr, kc, kr, x, j, layer, batch, post, wuv, wo, sm_scale):
    n_heads, T, kvl = ql.shape
    rope = qr.shape[-1]
    D = x.shape[-1]
    S = T // batch
    tq = ATTN_TQ
    nq = S // tq
    vd = wuv.shape[-1]
    rows = n_heads * tq
    return pl.pallas_call(
        functools.partial(_attn_prompt_body, sm_scale),
        grid=(batch, nq),
        in_specs=[
            pl.BlockSpec((n_heads, tq, kvl), lambda b, q: (0, b * nq + q, 0)),
            pl.BlockSpec((n_heads, tq, rope), lambda b, q: (0, b * nq + q, 0)),
            pl.BlockSpec((S, kvl), lambda b, q: (b, 0)),
            pl.BlockSpec((S, rope), lambda b, q: (b, 0)),
            pl.BlockSpec((tq, D), lambda b, q: (b * nq + q, 0)),
            pl.BlockSpec((None, 1, D), lambda b, q: (layer, 0, 0)),
            pl.BlockSpec((None, n_heads, kvl, vd), lambda b, q: (j, 0, 0, 0)),
            pl.BlockSpec((None, n_heads * vd, D), lambda b, q: (j, 0, 0)),
        ],
        out_specs=pl.BlockSpec((tq, D), lambda b, q: (b * nq + q, 0)),
        out_shape=jax.ShapeDtypeStruct((T, D), F32),
        scratch_shapes=[
            pltpu.VMEM((rows, LANES), F32),
            pltpu.VMEM((rows, LANES), F32),
            pltpu.VMEM((rows, kvl), F32),
        ],
        compiler_params=_params("parallel", "arbitrary"),
        name="attn_prompt",
    )(ql, qr, kc, kr, x, post, wuv, wo)


def _attn_sample_body(sm_scale, layer_idx, pt_ref, ql_ref, qr_ref, cn_ref, rn_ref, kv_hbm, krt_hbm,
                      o_ref, kv_buf, krt_buf, kc_scr, krt_scr, sem):
    b = pl.program_id(0)
    n_seq = pl.num_programs(0)
    n_pages, page = kv_buf.shape[1:3]
    slot = b % 2

    def page_copies(pid, buf_slot, k):
        return (pltpu.make_async_copy(kv_hbm.at[layer_idx, pid], kv_buf.at[buf_slot, k], sem.at[0, buf_slot]),
                pltpu.make_async_copy(krt_hbm.at[layer_idx, pid], krt_buf.at[buf_slot, k], sem.at[1, buf_slot]))

    def start_fetch(seq, buf_slot):
        for k in range(n_pages):
            for copy in page_copies(pt_ref[seq * n_pages + k], buf_slot, k):
                copy.start()

    @pl.when(b == 0)
    def _():
        start_fetch(0, 0)

    @pl.when(b + 1 < n_seq)
    def _():
        start_fetch(b + 1, 1 - slot)

    for k in range(n_pages):
        for copy in page_copies(0, slot, k):
            copy.wait()

    for k in range(n_pages):
        kc_scr[k * page:(k + 1) * page, :] = kv_buf[slot, k].astype(BF16)
        krt_scr[:, k * page:(k + 1) * page] = krt_buf[slot, k].astype(BF16)

    ql = ql_ref[...]
    qr = qr_ref[...]
    s = _dot_nt(ql, kc_scr[...]) + _dot(qr, krt_scr[...])

    tt = cn_ref.shape[0]
    qlf = ql.astype(F32)
    qrf = qr.astype(F32)
    q_t = lax.broadcasted_iota(jnp.int32, (ql.shape[0], 1), 0) % tt
    c_new = [cn_ref[t:t + 1, :].astype(BF16).astype(F32) for t in range(tt)]
    r_new = [rn_ref[t:t + 1, :].astype(BF16).astype(F32) for t in range(tt)]
    s_new = [jnp.sum(qlf * c_new[t], axis=-1, keepdims=True) + jnp.sum(qrf * r_new[t], axis=-1, keepdims=True)
             for t in range(tt)]

    m = jnp.max(s, axis=-1, keepdims=True)
    for t in range(tt):
        m = jnp.where(q_t >= t, jnp.maximum(m, s_new[t]), m)
    c_exp = sm_scale * LOG2_E
    p = jnp.exp2((s - m) * c_exp)
    l = jnp.sum(p, axis=-1, keepdims=True)
    acc = _dot(p.astype(BF16), kc_scr[...])
    for t in range(tt):
        p_t = jnp.where(q_t >= t, jnp.exp2((s_new[t] - m) * c_exp), 0.0)
        l = l + p_t
        acc = acc + p_t * c_new[t]
    o_ref[...] = acc / l


def _attn_sample(ql, qr, c_new, r_new, cache_kv, cache_krt, page_table, j, sm_scale):
    B, rows, kvl = ql.shape
    rope = qr.shape[-1]
    tt = c_new.shape[1]
    page = cache_kv.shape[2]
    n_pages = page_table.shape[1]
    pt = page_table.reshape(-1)

    per_b = lambda w: pl.BlockSpec((None, rows, w), lambda b, pt_ref: (b, 0, 0))
    new_b = lambda w: pl.BlockSpec((None, tt, w), lambda b, pt_ref: (b, 0, 0))
    hbm = pl.BlockSpec(memory_space=pl.ANY)
    return pl.pallas_call(
        functools.partial(_attn_sample_body, sm_scale, j),
        grid_spec=pltpu.PrefetchScalarGridSpec(
            num_scalar_prefetch=1,
            grid=(B,),
            in_specs=[per_b(kvl), per_b(rope), new_b(kvl), new_b(rope), hbm, hbm],
            out_specs=pl.BlockSpec((None, rows, kvl), lambda b, pt_ref: (b, 0, 0)),
            scratch_shapes=[
                pltpu.VMEM((2, n_pages, page, kvl), F32),
                pltpu.VMEM((2, n_pages, rope, page), F32),
                pltpu.VMEM((n_pages * page, kvl), BF16),
                pltpu.VMEM((rope, n_pages * page), BF16),
                pltpu.SemaphoreType.DMA((2, 2)),
            ],
        ),
        out_shape=jax.ShapeDtypeStruct((B, rows, kvl), F32),
        compiler_params=_params("arbitrary"),
        name="attn_sample",
    )(pt, ql, qr, c_new, r_new, cache_kv, cache_krt)


def _mla_out_sample_body(oh_ref, x_ref, post_ref, wuv_ref, wo_ref, o_ref):
    o_heads = [oh_ref[hd] for hd in range(oh_ref.shape[0])]
    o_ref[...] = _mla_out(o_heads, x_ref[...], post_ref[...], wuv_ref, wo_ref)


def _mla_out_sample(oh, x, j, layer, post, wuv, wo):
    n_heads, T, kvl = oh.shape
    D = x.shape[-1]
    vd = wuv.shape[-1]
    return pl.pallas_call(
        _mla_out_sample_body,
        grid=(1,),
        in_specs=[
            pl.BlockSpec((n_heads, T, kvl), lambda g: (0, 0, 0)),
            pl.BlockSpec((T, D), lambda g: (0, 0)),
            pl.BlockSpec((None, 1, D), lambda g: (layer, 0, 0)),
            pl.BlockSpec((None, n_heads, kvl, vd), lambda g: (j, 0, 0, 0)),
            pl.BlockSpec((None, n_heads * vd, D), lambda g: (j, 0, 0)),
        ],
        out_specs=pl.BlockSpec((T, D), lambda g: (0, 0)),
        out_shape=jax.ShapeDtypeStruct((T, D), F32),
        compiler_params=_params("arbitrary"),
        name="mla_out_sample",
    )(oh, x, post, wuv, wo)


def _rope_tables(pos, rope, n_heads):
    half = rope // 2
    inv_freq = 1.0 / (ROPE_THETA ** (jnp.arange(half, dtype=F32) * (2.0 / rope)))
    ang = pos.astype(F32)[:, None] * inv_freq[None, :]
    cos, sin = jnp.cos(ang), jnp.sin(ang)
    cos_k = jnp.concatenate([cos, cos], axis=-1)
    sin_k = jnp.concatenate([-sin, sin], axis=-1)
    return jnp.tile(cos_k, (1, n_heads)), jnp.tile(sin_k, (1, n_heads)), cos_k, sin_k


def kernel(x_prompt, x_sample, state_conv_a, state_conv_b, cache_kv_latent, cache_k_rope, page_table,
           mix_pre_gain, mix_post_gain, ffn_pre_gain, ffn_post_gain, w_ffn_gate, w_ffn_up, w_ffn_down,
           w_in_conv, conv_a_w, conv_a_b, ln_a_gain, ln_a_bias, conv_b_w, w_out_conv,
           w_in_mla, q_norm_gain, w_uq, kv_norm_gain, w_uk, w_uv, w_o_mla):
    Bp, Sp, D = x_prompt.shape
    Bs, Ss, _ = x_sample.shape
    depth = mix_pre_gain.shape[0]
    n_mla = w_in_mla.shape[0]
    kvl, n_heads, nope = w_uk.shape[1:]
    ql = q_norm_gain.shape[-1]
    rope = w_in_mla.shape[-1] - ql - kvl
    sm_scale = float(nope + rope) ** -0.5
    past_len = page_table.shape[1] * cache_kv_latent.shape[2]

    row3 = lambda a: a.reshape(a.shape[0], 1, a.shape[1])
    mix_pre, mix_post = row3(mix_pre_gain), row3(mix_post_gain)
    ffn_pre, ffn_post = row3(ffn_pre_gain), row3(ffn_post_gain)
    wg, wu, wd = w_ffn_gate.astype(BF16), w_ffn_up.astype(BF16), w_ffn_down.astype(BF16)
    win_c, wout_c = w_in_conv.astype(BF16), w_out_conv.astype(BF16)
    ba, lng, lnb = row3(conv_a_b), row3(ln_a_gain), row3(ln_a_bias)
    win_m = w_in_mla.astype(BF16)
    qg, kvg = row3(q_norm_gain), row3(kv_norm_gain)
    wuq4 = w_uq.reshape(n_mla, ql, n_heads, nope + rope)
    wuqn = wuq4[..., :nope].reshape(n_mla, ql, n_heads * nope).astype(BF16)
    wuqr = wuq4[..., nope:].reshape(n_mla, ql, n_heads * rope).astype(BF16)
    wukt = jnp.transpose(w_uk, (0, 2, 3, 1)).astype(BF16)
    wuv = jnp.transpose(w_uv, (0, 2, 1, 3)).astype(BF16)
    wo = w_o_mla.astype(BF16)

    tab_p = _rope_tables(jnp.arange(Sp), rope, n_heads)
    tab_s = _rope_tables(past_len + jnp.repeat(jnp.arange(Ss), Bs), rope, n_heads)

    yp = x_prompt.reshape(Bp * Sp, D)
    ys = jnp.transpose(x_sample, (1, 0, 2))
    cache_krt = jnp.swapaxes(cache_k_rope, 2, 3)
    sa_tm = jnp.transpose(state_conv_a, (0, 2, 1, 3))
    sb_tm = jnp.transpose(state_conv_b, (0, 2, 1, 3))

    ca_p, ca_s, cb_p, cb_s = [], [], [], []
    kv_p, kv_s, kr_p, kr_s = [], [], [], []
    for layer in range(depth):
        if layer % 2 == 0:
            i = layer // 2
            prm = (mix_pre, mix_post, win_c, conv_a_w, ba, lng, lnb, conv_b_w, wout_c)
            yp, na_p, nb_p = _conv_prompt(yp, i, layer, Bp, *prm)
            ys, na_s, nb_s = _conv_sample(ys, sa_tm, sb_tm, i, layer, *prm)
            ca_p.append(na_p); cb_p.append(nb_p); ca_s.append(na_s); cb_s.append(nb_s)
        else:
            j = layer // 2
            prm = (mix_pre, win_m, qg, wuqn, wuqr, kvg, wukt)
            qlp, qrp, c_p, r_p, cb16, rb16 = _mla_proj(yp, j, layer, *prm, *tab_p)
            yp = _attn_prompt(qlp, qrp, cb16, rb16, yp, j, layer, Bp, mix_post, wuv, wo, sm_scale)
            ys2 = ys.reshape(Ss * Bs, D)
            qls, qrs, c_s, r_s, _, _ = _mla_proj(ys2, j, layer, *prm, *tab_s)
            to_b = lambda a: jnp.transpose(a.reshape(n_heads, Ss, Bs, a.shape[-1]), (2, 0, 1, 3)).reshape(
                Bs, n_heads * Ss, a.shape[-1])
            c_sb = jnp.transpose(c_s.reshape(Ss, Bs, kvl), (1, 0, 2))
            r_sb = jnp.transpose(r_s.reshape(Ss, Bs, rope), (1, 0, 2))
            o_lat = _attn_sample(to_b(qls), to_b(qrs), c_sb, r_sb, cache_kv_latent, cache_krt,
                                 page_table, j, sm_scale)
            oh = jnp.transpose(o_lat.reshape(Bs, n_heads, Ss, kvl), (1, 2, 0, 3)).reshape(n_heads, Ss * Bs, kvl)
            ys = _mla_out_sample(oh, ys2, j, layer, mix_post, wuv, wo).reshape(Ss, Bs, D)
            kv_p.append(c_p.reshape(Bp, Sp, kvl)); kr_p.append(r_p.reshape(Bp, Sp, rope))
            kv_s.append(c_sb); kr_s.append(r_sb)
        yp = _ffn(yp, layer, ffn_pre, ffn_post, wg, wu, wd)
        ys = _ffn(ys.reshape(Ss * Bs, D), layer, ffn_pre, ffn_post, wg, wu, wd).reshape(Ss, Bs, D)

    from_tm = lambda xs: jnp.transpose(jnp.stack(xs), (0, 2, 1, 3))
    return (yp.reshape(Bp, Sp, D), jnp.transpose(ys, (1, 0, 2)),
            jnp.stack(ca_p), from_tm(ca_s), jnp.stack(cb_p), from_tm(cb_s),
            jnp.stack(kv_p), jnp.stack(kv_s), jnp.stack(kr_p), jnp.stack(kr_s))
```

```python
import functools

import jax
import jax.numpy as jnp
from jax import lax
from jax.experimental import pallas as pl
from jax.experimental.pallas import tpu as pltpu

F32 = jnp.float32
BF16 = jnp.bfloat16

ROPE_THETA = 10000.0
RMS_EPS = 1e-6
LN_EPS = 1e-5
LOG2_E = 1.4426950408889634

V7X_VMEM_LIMIT_BYTES = 56 * 1024 * 1024
SUBLANES = 8
LANES = 128

FFN_TM = 1024
FFN_TF = 256
CONV_TS = 512
CONV_TILES_PER_STEP = 2
CONV_RC = 64
CONV_HIST = 32
CONV_SAMPLE_NB = 64
PROJ_TM = 512
ATTN_TQ = 256
SAMPLE_KEY_CHUNKS = 2


def _params(*sem):
    return pltpu.CompilerParams(dimension_semantics=sem, vmem_limit_bytes=V7X_VMEM_LIMIT_BYTES)


def _rmsnorm(x, g):
    return x * lax.rsqrt(jnp.mean(x * x, axis=-1, keepdims=True) + RMS_EPS) * g


def _layernorm(x, g, b):
    mu = jnp.mean(x, axis=-1, keepdims=True)
    xc = x - mu
    var = jnp.mean(xc * xc, axis=-1, keepdims=True)
    return xc * lax.rsqrt(var + LN_EPS) * g + b


def _silu(x):
    return x * jax.nn.sigmoid(x)


def _dot(a, b):
    return jnp.dot(a, b, preferred_element_type=F32)


def _dot_nt(a, b):
    return lax.dot_general(a, b, (((1,), (1,)), ((), ())), preferred_element_type=F32)


def _ffn_body(x_ref, pre_ref, post_ref, wg_ref, wu_ref, wd_ref, o_ref, h_scr, act_scr):
    h_scr[...] = _rmsnorm(x_ref[...], pre_ref[...]).astype(BF16)
    for c in range(wg_ref.shape[-1] // FFN_TF):
        cols = slice(c * FFN_TF, (c + 1) * FFN_TF)
        gate = _dot(h_scr[...], wg_ref[:, cols])
        up = _dot(h_scr[...], wu_ref[:, cols])
        act_scr[:, cols] = (_silu(gate) * up).astype(BF16)
    o_ref[...] = x_ref[...] + _rmsnorm(_dot(act_scr[...], wd_ref[...]), post_ref[...])


def _ffn(x, layer, pre, post, wg, wu, wd):
    T, D = x.shape
    F = wg.shape[-1]
    tm = min(FFN_TM, T)
    gain_spec = pl.BlockSpec((None, 1, D), lambda i: (layer, 0, 0))
    resident = dict(pipeline_mode=pl.Buffered(1))
    return pl.pallas_call(
        _ffn_body,
        grid=(T // tm,),
        in_specs=[
            pl.BlockSpec((tm, D), lambda i: (i, 0)),
            gain_spec, gain_spec,
            pl.BlockSpec((None, D, F), lambda i: (layer, 0, 0), **resident),
            pl.BlockSpec((None, D, F), lambda i: (layer, 0, 0), **resident),
            pl.BlockSpec((None, F, D), lambda i: (layer, 0, 0), **resident),
        ],
        out_specs=pl.BlockSpec((tm, D), lambda i: (i, 0)),
        out_shape=jax.ShapeDtypeStruct((T, D), F32),
        scratch_shapes=[pltpu.VMEM((tm, D), BF16), pltpu.VMEM((tm, F), BF16)],
        compiler_params=_params("parallel"),
        name="ffn",
    )(x, pre, post, wg, wu, wd)


def _conv_prompt_body(x_ref, pre_ref, post_ref, win_ref, wa_ref, ba_ref, lng_ref, lnb_ref, wb_ref, wout_ref,
                      o_ref, sa_ref, sb_ref, u0_scr, u1_scr, apad_scr, shift_scr, zpad_scr, cat_scr):
    t = pl.program_id(1)
    rows = x_ref.shape[0]
    ts = u0_scr.shape[0]
    ca = apad_scr.shape[1]
    wa_n = wa_ref.shape[0]
    wb_n = wb_ref.shape[0]
    u_scrs = (u0_scr, u1_scr)
    a0 = CONV_HIST - (wa_n - 1)
    z0 = SUBLANES - (wb_n - 1)
    n_shift_rows = shift_scr.shape[1]

    @pl.when(t == 0)
    def _():
        apad_scr[0:CONV_HIST, :] = jnp.zeros((CONV_HIST, ca), F32)
        zpad_scr[0:SUBLANES, :] = jnp.zeros((SUBLANES, ca), F32)

    def project(tile):
        xs = x_ref[tile * ts:(tile + 1) * ts, :]
        u_scrs[tile % 2][...] = _dot(_rmsnorm(xs, pre_ref[...]).astype(BF16), win_ref[...])

    def mix(tile):
        u_scr = u_scrs[tile % 2]
        base = tile * ts
        apad_scr[CONV_HIST + base:CONV_HIST + base + ts, :] = u_scr[:, 0:ca] * jax.nn.sigmoid(u_scr[:, ca:2 * ca])
        zpad_scr[SUBLANES + base:SUBLANES + base + ts, :] = u_scr[:, 3 * ca:4 * ca] * u_scr[:, 4 * ca:5 * ca]
        for r in range(1, SUBLANES):
            shift_scr[r - 1] = apad_scr[base + r:base + r + n_shift_rows, :]

        def tap_rows(r0, k):
            off = a0 + k
            lo, r = off - off % SUBLANES, off % SUBLANES
            if r == 0:
                return apad_scr[base + r0 + lo:base + r0 + lo + CONV_RC, :]
            return shift_scr[r - 1, r0 + lo:r0 + lo + CONV_RC, :]

        for c in range(ts // CONV_RC):
            r0 = c * CONV_RC
            acc = jnp.broadcast_to(ba_ref[...], (CONV_RC, ca))
            for k in range(wa_n):
                w_rows = jnp.concatenate([wa_ref[k]] * (CONV_RC // SUBLANES), axis=0)
                acc = acc + w_rows * tap_rows(r0, k)
            a = _silu(_layernorm(acc, lng_ref[...], lnb_ref[...]))
            cat_scr[r0:r0 + CONV_RC, 0:ca] = a.astype(BF16)
            zr = base + r0 + z0
            cb = wb_ref[0:1, :] * zpad_scr[zr:zr + CONV_RC, :]
            for k in range(1, wb_n):
                cb = cb + wb_ref[k:k + 1, :] * zpad_scr[zr + k:zr + k + CONV_RC, :]
            cat_scr[r0:r0 + CONV_RC, ca:2 * ca] = (u_scr[r0:r0 + CONV_RC, 2 * ca:3 * ca] * cb).astype(BF16)

        out = _dot(cat_scr[...], wout_ref[...])
        o_ref[base:base + ts, :] = x_ref[base:base + ts, :] + _rmsnorm(out, post_ref[...])

    n_tiles = rows // ts
    project(0)
    for tile in range(n_tiles):
        if tile + 1 < n_tiles:
            project(tile + 1)
        mix(tile)

    @pl.when(t == pl.num_programs(1) - 1)
    def _():
        sa_ref[...] = apad_scr[CONV_HIST + rows - (wa_n - 1):CONV_HIST + rows, :]
        sb_ref[...] = zpad_scr[SUBLANES + rows - (wb_n - 1):SUBLANES + rows, :]

    apad_scr[0:CONV_HIST, :] = apad_scr[rows:rows + CONV_HIST, :]
    zpad_scr[0:SUBLANES, :] = zpad_scr[rows:rows + SUBLANES, :]


def _conv_prompt(x, i, layer, batch, pre, post, win, wa, ba, lng, lnb, wb, wout):
    T, D = x.shape
    S = T // batch
    ts = CONV_TS
    rows = CONV_TILES_PER_STEP * ts
    nt = S // rows
    ca = wa.shape[-1]
    wa_n, wb_n = wa.shape[1], wb.shape[1]
    assert wa_n - 1 <= CONV_HIST and wb_n - 1 <= SUBLANES
    gain_spec = pl.BlockSpec((None, 1, D), lambda b, t: (layer, 0, 0))
    vec_spec = pl.BlockSpec((None, 1, ca), lambda b, t: (i, 0, 0))
    return pl.pallas_call(
        _conv_prompt_body,
        grid=(batch, nt),
        in_specs=[
            pl.BlockSpec((rows, D), lambda b, t: (b * nt + t, 0)),
            gain_spec, gain_spec,
            pl.BlockSpec((None, D, 5 * ca), lambda b, t: (i, 0, 0)),
            pl.BlockSpec((None, wa_n, SUBLANES, ca), lambda b, t: (i, 0, 0, 0)),
            vec_spec, vec_spec, vec_spec,
            pl.BlockSpec((None, wb_n, ca), lambda b, t: (i, 0, 0)),
            pl.BlockSpec((None, 2 * ca, D), lambda b, t: (i, 0, 0)),
        ],
        out_specs=[
            pl.BlockSpec((rows, D), lambda b, t: (b * nt + t, 0)),
            pl.BlockSpec((None, wa_n - 1, ca), lambda b, t: (b, 0, 0)),
            pl.BlockSpec((None, wb_n - 1, ca), lambda b, t: (b, 0, 0)),
        ],
        out_shape=[
            jax.ShapeDtypeStruct((T, D), F32),
            jax.ShapeDtypeStruct((batch, wa_n - 1, ca), F32),
            jax.ShapeDtypeStruct((batch, wb_n - 1, ca), F32),
        ],
        scratch_shapes=[
            pltpu.VMEM((ts, 5 * ca), F32),
            pltpu.VMEM((ts, 5 * ca), F32),
            pltpu.VMEM((CONV_HIST + rows, ca), F32),
            pltpu.VMEM((SUBLANES - 1, CONV_HIST + ts - SUBLANES, ca), F32),
            pltpu.VMEM((SUBLANES + rows, ca), F32),
            pltpu.VMEM((ts, 2 * ca), BF16),
        ],
        compiler_params=_params("parallel", "arbitrary"),
        name="conv_prompt",
    )(x, pre, post, win, wa, ba, lng, lnb, wb, wout)


def _conv_sample_body(x_ref, sa_ref, sb_ref, pre_ref, post_ref, win_ref, wa_ref, ba_ref, lng_ref, lnb_ref,
                      wb_ref, wout_ref, o_ref, nsa_ref, nsb_ref, u_scr, glu_scr, z_scr, cat_scr):
    tt, nb, d = x_ref.shape
    ca = glu_scr.shape[-1]
    wa_n = wa_ref.shape[0]
    wb_n = wb_ref.shape[0]
    x = x_ref[...].reshape(tt * nb, d)
    h = _rmsnorm(x, pre_ref[...]).astype(BF16)
    u_scr[...] = _dot(h, win_ref[...])
    for t in range(tt):
        rows = slice(t * nb, (t + 1) * nb)
        glu_scr[t] = u_scr[rows, 0:ca] * jax.nn.sigmoid(u_scr[rows, ca:2 * ca])
        z_scr[t] = u_scr[rows, 3 * ca:4 * ca] * u_scr[rows, 4 * ca:5 * ca]

    def apad(j, rs):
        return sa_ref[j, rs, :] if j < wa_n - 1 else glu_scr[j - (wa_n - 1), rs, :]

    def zpad(j, rs):
        return sb_ref[j, rs, :] if j < wb_n - 1 else z_scr[j - (wb_n - 1), rs, :]

    for t in range(tt):
        for r in range(nb // CONV_RC):
            rs = slice(r * CONV_RC, (r + 1) * CONV_RC)
            orows = slice(t * nb + r * CONV_RC, t * nb + (r + 1) * CONV_RC)
            acc = jnp.broadcast_to(ba_ref[...], (CONV_RC, ca))
            for k in range(wa_n):
                acc = acc + wa_ref[k:k + 1, :] * apad(t + k, rs)
            a = _silu(_layernorm(acc, lng_ref[...], lnb_ref[...]))
            cat_scr[orows, 0:ca] = a.astype(BF16)
            cb = wb_ref[0:1, :] * zpad(t, rs)
            for k in range(1, wb_n):
                cb = cb + wb_ref[k:k + 1, :] * zpad(t + k, rs)
            cat_scr[orows, ca:2 * ca] = (u_scr[orows, 2 * ca:3 * ca] * cb).astype(BF16)

    out = _dot(cat_scr[...], wout_ref[...])
    o_ref[...] = (x + _rmsnorm(out, post_ref[...])).reshape(tt, nb, d)
    full = slice(0, nb)
    for j in range(wa_n - 1):
        nsa_ref[j] = apad(j + tt, full)
    for j in range(wb_n - 1):
        nsb_ref[j] = zpad(j + tt, full)


def _conv_sample(x, sa, sb, i, layer, pre, post, win, wa, ba, lng, lnb, wb, wout):
    tt, B, D = x.shape
    nb = CONV_SAMPLE_NB
    ca = wa.shape[-1]
    wa_n, wb_n = wa.shape[1], wb.shape[1]
    gain_spec = pl.BlockSpec((None, 1, D), lambda g: (layer, 0, 0))
    vec_spec = pl.BlockSpec((None, 1, ca), lambda g: (i, 0, 0))
    return pl.pallas_call(
        _conv_sample_body,
        grid=(B // nb,),
        in_specs=[
            pl.BlockSpec((tt, nb, D), lambda g: (0, g, 0)),
            pl.BlockSpec((None, wa_n - 1, nb, ca), lambda g: (i, 0, g, 0)),
            pl.BlockSpec((None, wb_n - 1, nb, ca), lambda g: (i, 0, g, 0)),
            gain_spec, gain_spec,
            pl.BlockSpec((None, D, 5 * ca), lambda g: (i, 0, 0)),
            pl.BlockSpec((None, wa_n, ca), lambda g: (i, 0, 0)),
            vec_spec, vec_spec, vec_spec,
            pl.BlockSpec((None, wb_n, ca), lambda g: (i, 0, 0)),
            pl.BlockSpec((None, 2 * ca, D), lambda g: (i, 0, 0)),
        ],
        out_specs=[
            pl.BlockSpec((tt, nb, D), lambda g: (0, g, 0)),
            pl.BlockSpec((wa_n - 1, nb, ca), lambda g: (0, g, 0)),
            pl.BlockSpec((wb_n - 1, nb, ca), lambda g: (0, g, 0)),
        ],
        out_shape=[
            jax.ShapeDtypeStruct((tt, B, D), F32),
            jax.ShapeDtypeStruct((wa_n - 1, B, ca), F32),
            jax.ShapeDtypeStruct((wb_n - 1, B, ca), F32),
        ],
        scratch_shapes=[
            pltpu.VMEM((tt * nb, 5 * ca), F32),
            pltpu.VMEM((tt, nb, ca), F32),
            pltpu.VMEM((tt, nb, ca), F32),
            pltpu.VMEM((tt * nb, 2 * ca), BF16),
        ],
        compiler_params=_params("parallel"),
        name="conv_sample",
    )(x, sa, sb, pre, post, win, wa, ba, lng, lnb, wb, wout)


def _mla_proj_body(x_ref, pre_ref, win_ref, qg_ref, wuqn_ref, wuqr_ref, kvg_ref, wukt_ref,
                   cosq_ref, sinq_ref, cosk_ref, sink_ref,
                   qlat_ref, qrope_ref, ckv_ref, kr_ref, ckvb_ref, krb_ref):
    n_heads, _, kvl = qlat_ref.shape
    rope = qrope_ref.shape[-1]
    half = rope // 2
    ql = qg_ref.shape[-1]
    nope = wuqn_ref.shape[-1] // n_heads

    h = _rmsnorm(x_ref[...], pre_ref[...]).astype(BF16)
    u = _dot(h, win_ref[...])
    cq, ckv, kr = u[:, :ql], u[:, ql:ql + kvl], u[:, ql + kvl:]
    qn = _rmsnorm(cq, qg_ref[...]).astype(BF16)
    q_nope = _dot(qn, wuqn_ref[...])
    q_r = _dot(qn, wuqr_ref[...])
    width = n_heads * rope
    lane = lax.broadcasted_iota(jnp.int32, q_r.shape, 1)
    swapped = jnp.where(lane % rope < half,
                        pltpu.roll(q_r, width - half, 1), pltpu.roll(q_r, half, 1))
    q_rot = q_r * cosq_ref[...] + swapped * sinq_ref[...]
    for hd in range(n_heads):
        q_h = q_nope[:, hd * nope:(hd + 1) * nope].astype(BF16)
        qlat_ref[hd] = _dot(q_h, wukt_ref[hd]).astype(BF16)
        qrope_ref[hd] = q_rot[:, hd * rope:(hd + 1) * rope].astype(BF16)

    c = _rmsnorm(ckv, kvg_ref[...])
    ckv_ref[...] = c
    ckvb_ref[...] = c.astype(BF16)
    kr_swapped = jnp.concatenate([kr[:, half:], kr[:, :half]], axis=-1)
    kr_rot = kr * cosk_ref[...] + kr_swapped * sink_ref[...]
    kr_ref[...] = kr_rot
    krb_ref[...] = kr_rot.astype(BF16)


def _mla_proj(x, j, layer, pre, win, qg, wuqn, wuqr, kvg, wukt, cosq, sinq, cosk, sink):
    T, D = x.shape
    tm = min(PROJ_TM, T)
    n_pos = cosq.shape[0] // tm
    n_heads, nope, kvl = wukt.shape[1:]
    ql = qg.shape[-1]
    rope = cosk.shape[-1]
    din = win.shape[-1]
    tok = lambda w: pl.BlockSpec((tm, w), lambda i: (i, 0))
    pos = lambda w: pl.BlockSpec((tm, w), lambda i: (i % n_pos, 0))
    return pl.pallas_call(
        _mla_proj_body,
        grid=(T // tm,),
        in_specs=[
            tok(D),
            pl.BlockSpec((None, 1, D), lambda i: (layer, 0, 0)),
            pl.BlockSpec((None, D, din), lambda i: (j, 0, 0)),
            pl.BlockSpec((None, 1, ql), lambda i: (j, 0, 0)),
            pl.BlockSpec((None, ql, n_heads * nope), lambda i: (j, 0, 0)),
            pl.BlockSpec((None, ql, n_heads * rope), lambda i: (j, 0, 0)),
            pl.BlockSpec((None, 1, kvl), lambda i: (j, 0, 0)),
            pl.BlockSpec((None, n_heads, nope, kvl), lambda i: (j, 0, 0, 0)),
            pos(n_heads * rope), pos(n_heads * rope), pos(rope), pos(rope),
        ],
        out_specs=[
            pl.BlockSpec((n_heads, tm, kvl), lambda i: (0, i, 0)),
            pl.BlockSpec((n_heads, tm, rope), lambda i: (0, i, 0)),
            tok(kvl), tok(rope), tok(kvl), tok(rope),
        ],
        out_shape=[
            jax.ShapeDtypeStruct((n_heads, T, kvl), BF16),
            jax.ShapeDtypeStruct((n_heads, T, rope), BF16),
            jax.ShapeDtypeStruct((T, kvl), F32),
            jax.ShapeDtypeStruct((T, rope), F32),
            jax.ShapeDtypeStruct((T, kvl), BF16),
            jax.ShapeDtypeStruct((T, rope), BF16),
        ],
        compiler_params=_params("parallel"),
        name="mla_proj",
    )(x, pre, win, qg, wuqn, wuqr, kvg, wukt, cosq, sinq, cosk, sink)


def _mla_out(o_heads, x, post, wuv_ref, wo_ref):
    cat = jnp.concatenate(
        [_dot(o.astype(BF16), wuv_ref[hd]).astype(BF16) for hd, o in enumerate(o_heads)], axis=-1)
    return x + _rmsnorm(_dot(cat, wo_ref[...]), post)


def _attn_prompt_body(sm_scale, ql_ref, qr_ref, kc_ref, kr_ref, x_ref, post_ref, wuv_ref, wo_ref,
                      o_ref, m_scr, l_scr, acc_scr):
    qi = pl.program_id(1)
    n_heads, tq, kvl = ql_ref.shape
    rows = n_heads * tq
    lanes = m_scr.shape[-1]
    c_exp = sm_scale * LOG2_E
    ql = ql_ref[...].reshape(rows, kvl)
    qr = qr_ref[...].reshape(rows, qr_ref.shape[-1])

    def lane_groups(a):
        return [a[:, i * lanes:(i + 1) * lanes] for i in range(tq // lanes)]

    def scores(start, diagonal):
        s = _dot_nt(ql, kc_ref[pl.ds(start, tq), :]) + _dot_nt(qr, kr_ref[pl.ds(start, tq), :])
        if diagonal:
            q_pos = lax.broadcasted_iota(jnp.int32, s.shape, 0) % tq
            k_pos = lax.broadcasted_iota(jnp.int32, s.shape, 1)
            s = jnp.where(k_pos <= q_pos, s, -jnp.inf)
        return lane_groups(s)

    def update(s_parts, start, first):
        m_cur = jnp.max(functools.reduce(jnp.maximum, s_parts), axis=-1, keepdims=True)
        if first:
            m_new = jnp.broadcast_to(m_cur, (rows, lanes))
        else:
            m_prev = m_scr[...]
            m_new = jnp.maximum(m_prev, m_cur)
            alpha = jnp.exp2((m_prev - m_new) * c_exp)
        p_parts = [jnp.exp2((sp - m_new) * c_exp) for sp in s_parts]
        p = jnp.concatenate([pp.astype(BF16) for pp in p_parts], axis=-1)
        pv = _dot(p, kc_ref[pl.ds(start, tq), :])
        if first:
            l_scr[...] = functools.reduce(jnp.add, p_parts)
            acc_scr[...] = pv
        else:
            l_scr[...] = alpha * l_scr[...] + functools.reduce(jnp.add, p_parts)
            acc_scr[...] = jnp.concatenate([alpha] * (kvl // lanes), axis=-1) * acc_scr[...] + pv
        m_scr[...] = m_new

    diag_start = pl.multiple_of(qi * tq, tq)
    update(scores(diag_start, True), diag_start, True)

    def full_pair(j, carry):
        start_a = pl.multiple_of(2 * j * tq, tq)
        start_b = pl.multiple_of((2 * j + 1) * tq, tq)
        s_a = scores(start_a, False)
        s_b = scores(start_b, False)
        update(s_a, start_a, False)
        update(s_b, start_b, False)
        return carry

    lax.fori_loop(0, qi // 2, full_pair, 0)

    @pl.when(qi % 2 == 1)
    def _():
        start = pl.multiple_of((qi - 1) * tq, tq)
        update(scores(start, False), start, False)


    o_heads = []
    for hd in range(n_heads):
        rs = slice(hd * tq, (hd + 1) * tq)
        o_heads.append(acc_scr[rs, :] / jnp.sum(l_scr[rs, :], axis=-1, keepdims=True))
    o_ref[...] = _mla_out(o_heads, x_ref[...], post_ref[...], wuv_ref, wo_ref)


def _attn_prompt(ql, qr, kc, kr, x, j, layer, batch, post, wuv, wo, sm_scale):
    n_heads, T, kvl = ql.shape
    rope = qr.shape[-1]
    D = x.shape[-1]
    S = T // batch
    tq = ATTN_TQ
    nq = S // tq
    vd = wuv.shape[-1]
    rows = n_heads * tq
    return pl.pallas_call(
        functools.partial(_attn_prompt_body, sm_scale),
        grid=(batch, nq),
        in_specs=[
            pl.BlockSpec((n_heads, tq, kvl), lambda b, q: (0, b * nq + q, 0)),
            pl.BlockSpec((n_heads, tq, rope), lambda b, q: (0, b * nq + q, 0)),
            pl.BlockSpec((S, kvl), lambda b, q: (b, 0)),
            pl.BlockSpec((S, rope), lambda b, q: (b, 0)),
            pl.BlockSpec((tq, D), lambda b, q: (b * nq + q, 0)),
            pl.BlockSpec((None, 1, D), lambda b, q: (layer, 0, 0)),
            pl.BlockSpec((None, n_heads, kvl, vd), lambda b, q: (j, 0, 0, 0)),
            pl.BlockSpec((None, n_heads * vd, D), lambda b, q: (j, 0, 0)),
        ],
        out_specs=pl.BlockSpec((tq, D), lambda b, q: (b * nq + q, 0)),
        out_shape=jax.ShapeDtypeStruct((T, D), F32),
        scratch_shapes=[
            pltpu.VMEM((rows, LANES), F32),
            pltpu.VMEM((rows, LANES), F32),
            pltpu.VMEM((rows, kvl), F32),
        ],
        compiler_params=_params("parallel", "arbitrary"),
        name="attn_prompt",
    )(ql, qr, kc, kr, x, post, wuv, wo)


def _attn_sample_body(sm_scale, layer_idx, pt_ref, ql_ref, qr_ref, cn_ref, rn_ref, kv_hbm, krt_hbm,
                      o_ref, kv_buf, krt_buf, kc_scr, krt_scr, sem):
    b = pl.program_id(0)
    n_seq = pl.num_programs(0)
    n_pages, page = kv_buf.shape[1:3]
    slot = b % 2

    def page_copies(pid, buf_slot, k):
        return (pltpu.make_async_copy(kv_hbm.at[layer_idx, pid], kv_buf.at[buf_slot, k], sem.at[0, buf_slot]),
                pltpu.make_async_copy(krt_hbm.at[layer_idx, pid], krt_buf.at[buf_slot, k], sem.at[1, buf_slot]))

    def start_fetch(seq, buf_slot):
        for k in range(n_pages):
            for copy in page_copies(pt_ref[seq * n_pages + k], buf_slot, k):
                copy.start()

    @pl.when(b == 0)
    def _():
        start_fetch(0, 0)

    @pl.when(b + 1 < n_seq)
    def _():
        start_fetch(b + 1, 1 - slot)

    pltpu.make_async_copy(kv_buf.at[slot], kv_buf.at[slot], sem.at[0, slot]).wait()
    pltpu.make_async_copy(krt_buf.at[slot], krt_buf.at[slot], sem.at[1, slot]).wait()

    for k in range(n_pages):
        kc_scr[k * page:(k + 1) * page, :] = kv_buf[slot, k].astype(BF16)
        krt_scr[:, k * page:(k + 1) * page] = krt_buf[slot, k].astype(BF16)

    ql = ql_ref[...]
    qr = qr_ref[...]
    c_exp = sm_scale * LOG2_E

    def tree(op, parts):
        while len(parts) > 1:
            parts = [op(parts[i], parts[i + 1]) for i in range(0, len(parts) - 1, 2)] + parts[len(parts) & ~1:]
        return parts[0]

    def scores(lo, hi):
        s = _dot_nt(ql, kc_scr[lo:hi, :]) + _dot(qr, krt_scr[:, lo:hi])
        return [s[:, i * LANES:(i + 1) * LANES] for i in range(s.shape[-1] // LANES)]

    def weights(s_parts):
        m = jnp.max(tree(jnp.maximum, s_parts), axis=-1, keepdims=True)
        p_parts = [jnp.exp2((sp - m) * c_exp) for sp in s_parts]
        l = jnp.sum(tree(jnp.add, p_parts), axis=-1, keepdims=True)
        return m, l, jnp.concatenate([pp.astype(BF16) for pp in p_parts], axis=-1)

    n_keys = kc_scr.shape[0]
    bounds = [(i * n_keys // SAMPLE_KEY_CHUNKS, (i + 1) * n_keys // SAMPLE_KEY_CHUNKS)
              for i in range(SAMPLE_KEY_CHUNKS)]
    chunks = []
    s_next = scores(*bounds[0])
    for i, (lo, hi) in enumerate(bounds):
        s_cur = s_next
        if i + 1 < SAMPLE_KEY_CHUNKS:
            s_next = scores(*bounds[i + 1])
        m_c, l_c, p_c = weights(s_cur)
        chunks.append((m_c, l_c, _dot(p_c, kc_scr[lo:hi, :])))

    tt = cn_ref.shape[0]
    qlf = ql.astype(F32)
    qrf = qr.astype(F32)
    q_t = lax.broadcasted_iota(jnp.int32, (ql.shape[0], 1), 0) % tt
    c_new = [cn_ref[t:t + 1, :].astype(BF16).astype(F32) for t in range(tt)]
    r_new = [rn_ref[t:t + 1, :].astype(BF16).astype(F32) for t in range(tt)]
    s_new = [jnp.sum(qlf * c_new[t], axis=-1, keepdims=True) + jnp.sum(qrf * r_new[t], axis=-1, keepdims=True)
             for t in range(tt)]

    m = tree(jnp.maximum, [c[0] for c in chunks])
    for t in range(tt):
        m = jnp.where(q_t >= t, jnp.maximum(m, s_new[t]), m)
    l = jnp.zeros_like(m)
    acc = jnp.zeros_like(chunks[0][2])
    for m_c, l_c, acc_c in chunks:
        w = jnp.exp2((m_c - m) * c_exp)
        l = l + w * l_c
        acc = acc + w * acc_c
    for t in range(tt):
        p_t = jnp.where(q_t >= t, jnp.exp2((s_new[t] - m) * c_exp), 0.0)
        l = l + p_t
        acc = acc + p_t * c_new[t]
    o_ref[...] = acc / l


def _attn_sample(ql, qr, c_new, r_new, cache_kv, cache_krt, page_table, j, sm_scale):
    B, rows, kvl = ql.shape
    rope = qr.shape[-1]
    tt = c_new.shape[1]
    page = cache_kv.shape[2]
    n_pages = page_table.shape[1]
    pt = page_table.reshape(-1)

    per_b = lambda w: pl.BlockSpec((None, rows, w), lambda b, pt_ref: (b, 0, 0))
    new_b = lambda w: pl.BlockSpec((None, tt, w), lambda b, pt_ref: (b, 0, 0))
    hbm = pl.BlockSpec(memory_space=pl.ANY)
    return pl.pallas_call(
        functools.partial(_attn_sample_body, sm_scale, j),
        grid_spec=pltpu.PrefetchScalarGridSpec(
            num_scalar_prefetch=1,
            grid=(B,),
            in_specs=[per_b(kvl), per_b(rope), new_b(kvl), new_b(rope), hbm, hbm],
            out_specs=pl.BlockSpec((None, rows, kvl), lambda b, pt_ref: (b, 0, 0)),
            scratch_shapes=[
                pltpu.VMEM((2, n_pages, page, kvl), F32),
                pltpu.VMEM((2, n_pages, rope, page), F32),
                pltpu.VMEM((n_pages * page, kvl), BF16),
                pltpu.VMEM((rope, n_pages * page), BF16),
                pltpu.SemaphoreType.DMA((2, 2)),
            ],
        ),
        out_shape=jax.ShapeDtypeStruct((B, rows, kvl), F32),
        compiler_params=_params("arbitrary"),
        name="attn_sample",
    )(pt, ql, qr, c_new, r_new, cache_kv, cache_krt)


def _mla_out_sample_body(oh_ref, x_ref, post_ref, wuv_ref, wo_ref, o_ref):
    o_heads = [oh_ref[hd] for hd in range(oh_ref.shape[0])]
    o_ref[...] = _mla_out(o_heads, x_ref[...], post_ref[...], wuv_ref, wo_ref)


def _mla_out_sample(oh, x, j, layer, post, wuv, wo):
    n_heads, T, kvl = oh.shape
    D = x.shape[-1]
    vd = wuv.shape[-1]
    return pl.pallas_call(
        _mla_out_sample_body,
        grid=(1,),
        in_specs=[
            pl.BlockSpec((n_heads, T, kvl), lambda g: (0, 0, 0)),
            pl.BlockSpec((T, D), lambda g: (0, 0)),
            pl.BlockSpec((None, 1, D), lambda g: (layer, 0, 0)),
            pl.BlockSpec((None, n_heads, kvl, vd), lambda g: (j, 0, 0, 0)),
            pl.BlockSpec((None, n_heads * vd, D), lambda g: (j, 0, 0)),
        ],
        out_specs=pl.BlockSpec((T, D), lambda g: (0, 0)),
        out_shape=jax.ShapeDtypeStruct((T, D), F32),
        compiler_params=_params("arbitrary"),
        name="mla_out_sample",
    )(oh, x, post, wuv, wo)


def _rope_tables(pos, rope, n_heads):
    half = rope // 2
    inv_freq = 1.0 / (ROPE_THETA ** (jnp.arange(half, dtype=F32) * (2.0 / rope)))
    ang = pos.astype(F32)[:, None] * inv_freq[None, :]
    cos, sin = jnp.cos(ang), jnp.sin(ang)
    cos_k = jnp.concatenate([cos, cos], axis=-1)
    sin_k = jnp.concatenate([-sin, sin], axis=-1)
    return jnp.tile(cos_k, (1, n_heads)), jnp.tile(sin_k, (1, n_heads)), cos_k, sin_k


def kernel(x_prompt, x_sample, state_conv_a, state_conv_b, cache_kv_latent, cache_k_rope, page_table,
           mix_pre_gain, mix_post_gain, ffn_pre_gain, ffn_post_gain, w_ffn_gate, w_ffn_up, w_ffn_down,
           w_in_conv, conv_a_w, conv_a_b, ln_a_gain, ln_a_bias, conv_b_w, w_out_conv,
           w_in_mla, q_norm_gain, w_uq, kv_norm_gain, w_uk, w_uv, w_o_mla):
    Bp, Sp, D = x_prompt.shape
    Bs, Ss, _ = x_sample.shape
    depth = mix_pre_gain.shape[0]
    n_mla = w_in_mla.shape[0]
    kvl, n_heads, nope = w_uk.shape[1:]
    ql = q_norm_gain.shape[-1]
    rope = w_in_mla.shape[-1] - ql - kvl
    sm_scale = float(nope + rope) ** -0.5
    past_len = page_table.shape[1] * cache_kv_latent.shape[2]

    row3 = lambda a: a.reshape(a.shape[0], 1, a.shape[1])
    mix_pre, mix_post = row3(mix_pre_gain), row3(mix_post_gain)
    ffn_pre, ffn_post = row3(ffn_pre_gain), row3(ffn_post_gain)
    wg, wu, wd = w_ffn_gate.astype(BF16), w_ffn_up.astype(BF16), w_ffn_down.astype(BF16)
    win_c, wout_c = w_in_conv.astype(BF16), w_out_conv.astype(BF16)
    ba, lng, lnb = row3(conv_a_b), row3(ln_a_gain), row3(ln_a_bias)
    wa_tiles = jnp.broadcast_to(conv_a_w[:, :, None, :], conv_a_w.shape[:2] + (SUBLANES,) + conv_a_w.shape[2:])
    win_m = w_in_mla.astype(BF16)
    qg, kvg = row3(q_norm_gain), row3(kv_norm_gain)
    wuq4 = w_uq.reshape(n_mla, ql, n_heads, nope + rope)
    wuqn = wuq4[..., :nope].reshape(n_mla, ql, n_heads * nope).astype(BF16)
    wuqr = wuq4[..., nope:].reshape(n_mla, ql, n_heads * rope).astype(BF16)
    wukt = jnp.transpose(w_uk, (0, 2, 3, 1)).astype(BF16)
    wuv = jnp.transpose(w_uv, (0, 2, 1, 3)).astype(BF16)
    wo = w_o_mla.astype(BF16)

    tab_p = _rope_tables(jnp.arange(Sp), rope, n_heads)
    tab_s = _rope_tables(past_len + jnp.repeat(jnp.arange(Ss), Bs), rope, n_heads)

    yp = x_prompt.reshape(Bp * Sp, D)
    ys = jnp.transpose(x_sample, (1, 0, 2))
    cache_krt = jnp.swapaxes(cache_k_rope, 2, 3)
    sa_tm = jnp.transpose(state_conv_a, (0, 2, 1, 3))
    sb_tm = jnp.transpose(state_conv_b, (0, 2, 1, 3))

    ca_p, ca_s, cb_p, cb_s = [], [], [], []
    kv_p, kv_s, kr_p, kr_s = [], [], [], []
    for layer in range(depth):
        if layer % 2 == 0:
            i = layer // 2
            prm = (mix_pre, mix_post, win_c, conv_a_w, ba, lng, lnb, conv_b_w, wout_c)
            yp, na_p, nb_p = _conv_prompt(yp, i, layer, Bp, *prm[:3], wa_tiles, *prm[4:])
            ys, na_s, nb_s = _conv_sample(ys, sa_tm, sb_tm, i, layer, *prm)
            ca_p.append(na_p); cb_p.append(nb_p); ca_s.append(na_s); cb_s.append(nb_s)
        else:
            j = layer // 2
            prm = (mix_pre, win_m, qg, wuqn, wuqr, kvg, wukt)
            qlp, qrp, c_p, r_p, cb16, rb16 = _mla_proj(yp, j, layer, *prm, *tab_p)
            yp = _attn_prompt(qlp, qrp, cb16, rb16, yp, j, layer, Bp, mix_post, wuv, wo, sm_scale)
            ys2 = ys.reshape(Ss * Bs, D)
            qls, qrs, c_s, r_s, _, _ = _mla_proj(ys2, j, layer, *prm, *tab_s)
            to_b = lambda a: jnp.transpose(a.reshape(n_heads, Ss, Bs, a.shape[-1]), (2, 0, 1, 3)).reshape(
                Bs, n_heads * Ss, a.shape[-1])
            c_sb = jnp.transpose(c_s.reshape(Ss, Bs, kvl), (1, 0, 2))
            r_sb = jnp.transpose(r_s.reshape(Ss, Bs, rope), (1, 0, 2))
            o_lat = _attn_sample(to_b(qls), to_b(qrs), c_sb, r_sb, cache_kv_latent, cache_krt,
                                 page_table, j, sm_scale)
            oh = jnp.transpose(o_lat.reshape(Bs, n_heads, Ss, kvl), (1, 2, 0, 3)).reshape(n_heads, Ss * Bs, kvl)
            ys = _mla_out_sample(oh, ys2, j, layer, mix_post, wuv, wo).reshape(Ss, Bs, D)
            kv_p.append(c_p.reshape(Bp, Sp, kvl)); kr_p.append(r_p.reshape(Bp, Sp, rope))
            kv_s.append(c_sb); kr_s.append(r_sb)
        yp = _ffn(yp, layer, ffn_pre, ffn_post, wg, wu, wd)
        ys = _ffn(ys.reshape(Ss * Bs, D), layer, ffn_pre, ffn_post, wg, wu, wd).reshape(Ss, Bs, D)

    from_tm = lambda xs: jnp.transpose(jnp.stack(xs), (0, 2, 1, 3))
    return (yp.reshape(Bp, Sp, D), jnp.transpose(ys, (1, 0, 2)),
            jnp.stack(ca_p), from_tm(ca_s), jnp.stack(cb_p), from_tm(cb_s),
            jnp.stack(kv_p), jnp.stack(kv_s), jnp.stack(kr_p), jnp.stack(kr_s))
```

```python
import functools

import jax
import jax.numpy as jnp
from jax import lax
from jax.experimental import pallas as pl
from jax.experimental.pallas import tpu as pltpu

F32 = jnp.float32
BF16 = jnp.bfloat16

ROPE_THETA = 10000.0
RMS_EPS = 1e-6
LN_EPS = 1e-5
LOG2_E = 1.4426950408889634

V7X_VMEM_LIMIT_BYTES = 56 * 1024 * 1024
SUBLANES = 8
LANES = 128

FFN_TM = 1024
FFN_TF = 256
CONV_TS = 512
CONV_TILES_PER_STEP = 2
CONV_RC = 64
CONV_HIST = 32
CONV_SAMPLE_NB = 64
PROJ_TM = 1024
PROJ_SUB_TILES = 2
ATTN_TQ = 256
SAMPLE_KEY_CHUNKS = 2


def _params(*sem):
    return pltpu.CompilerParams(dimension_semantics=sem, vmem_limit_bytes=V7X_VMEM_LIMIT_BYTES)


def _rmsnorm(x, g):
    return x * lax.rsqrt(jnp.mean(x * x, axis=-1, keepdims=True) + RMS_EPS) * g


def _layernorm(x, g, b):
    mu = jnp.mean(x, axis=-1, keepdims=True)
    xc = x - mu
    var = jnp.mean(xc * xc, axis=-1, keepdims=True)
    return xc * lax.rsqrt(var + LN_EPS) * g + b


def _silu(x):
    return x * jax.nn.sigmoid(x)


def _dot(a, b):
    return jnp.dot(a, b, preferred_element_type=F32)


def _dot_nt(a, b):
    return lax.dot_general(a, b, (((1,), (1,)), ((), ())), preferred_element_type=F32)


def _ffn_body(x_ref, pre_ref, post_ref, wg_ref, wu_ref, wd_ref, o_ref, h_scr, act_scr):
    h_scr[...] = _rmsnorm(x_ref[...], pre_ref[...]).astype(BF16)
    for c in range(wg_ref.shape[-1] // FFN_TF):
        cols = slice(c * FFN_TF, (c + 1) * FFN_TF)
        gate = _dot(h_scr[...], wg_ref[:, cols])
        up = _dot(h_scr[...], wu_ref[:, cols])
        act_scr[:, cols] = (_silu(gate) * up).astype(BF16)
    o_ref[...] = x_ref[...] + _rmsnorm(_dot(act_scr[...], wd_ref[...]), post_ref[...])


def _ffn(x, layer, pre, post, wg, wu, wd):
    T, D = x.shape
    F = wg.shape[-1]
    tm = min(FFN_TM, T)
    gain_spec = pl.BlockSpec((None, 1, D), lambda i: (layer, 0, 0))
    resident = dict(pipeline_mode=pl.Buffered(1))
    return pl.pallas_call(
        _ffn_body,
        grid=(T // tm,),
        in_specs=[
            pl.BlockSpec((tm, D), lambda i: (i, 0)),
            gain_spec, gain_spec,
            pl.BlockSpec((None, D, F), lambda i: (layer, 0, 0), **resident),
            pl.BlockSpec((None, D, F), lambda i: (layer, 0, 0), **resident),
            pl.BlockSpec((None, F, D), lambda i: (layer, 0, 0), **resident),
        ],
        out_specs=pl.BlockSpec((tm, D), lambda i: (i, 0)),
        out_shape=jax.ShapeDtypeStruct((T, D), F32),
        scratch_shapes=[pltpu.VMEM((tm, D), BF16), pltpu.VMEM((tm, F), BF16)],
        compiler_params=_params("parallel"),
        name="ffn",
    )(x, pre, post, wg, wu, wd)


def _conv_prompt_body(x_ref, pre_ref, post_ref, win_ref, wa_ref, ba_ref, lng_ref, lnb_ref, wb_ref, wout_ref,
                      o_ref, sa_ref, sb_ref, u0_scr, u1_scr, apad_scr, shift_scr, zpad_scr, cat_scr):
    t = pl.program_id(1)
    rows = x_ref.shape[0]
    ts = u0_scr.shape[0]
    ca = apad_scr.shape[1]
    wa_n = wa_ref.shape[0]
    wb_n = wb_ref.shape[0]
    u_scrs = (u0_scr, u1_scr)
    a0 = CONV_HIST - (wa_n - 1)
    z0 = SUBLANES - (wb_n - 1)
    n_shift_rows = shift_scr.shape[1]

    @pl.when(t == 0)
    def _():
        apad_scr[0:CONV_HIST, :] = jnp.zeros((CONV_HIST, ca), F32)
        zpad_scr[0:SUBLANES, :] = jnp.zeros((SUBLANES, ca), F32)

    def project(tile):
        xs = x_ref[tile * ts:(tile + 1) * ts, :]
        u_scrs[tile % 2][...] = _dot(_rmsnorm(xs, pre_ref[...]).astype(BF16), win_ref[...])

    def mix(tile):
        u_scr = u_scrs[tile % 2]
        base = tile * ts
        apad_scr[CONV_HIST + base:CONV_HIST + base + ts, :] = u_scr[:, 0:ca] * jax.nn.sigmoid(u_scr[:, ca:2 * ca])
        zpad_scr[SUBLANES + base:SUBLANES + base + ts, :] = u_scr[:, 3 * ca:4 * ca] * u_scr[:, 4 * ca:5 * ca]
        for r in range(1, SUBLANES):
            shift_scr[r - 1] = apad_scr[base + r:base + r + n_shift_rows, :]

        def tap_rows(r0, k):
            off = a0 + k
            lo, r = off - off % SUBLANES, off % SUBLANES
            if r == 0:
                return apad_scr[base + r0 + lo:base + r0 + lo + CONV_RC, :]
            return shift_scr[r - 1, r0 + lo:r0 + lo + CONV_RC, :]

        for c in range(ts // CONV_RC):
            r0 = c * CONV_RC
            acc = jnp.broadcast_to(ba_ref[...], (CONV_RC, ca))
            for k in range(wa_n):
                w_rows = jnp.concatenate([wa_ref[k]] * (CONV_RC // SUBLANES), axis=0)
                acc = acc + w_rows * tap_rows(r0, k)
            a = _silu(_layernorm(acc, lng_ref[...], lnb_ref[...]))
            cat_scr[r0:r0 + CONV_RC, 0:ca] = a.astype(BF16)
            zr = base + r0 + z0
            cb = wb_ref[0:1, :] * zpad_scr[zr:zr + CONV_RC, :]
            for k in range(1, wb_n):
                cb = cb + wb_ref[k:k + 1, :] * zpad_scr[zr + k:zr + k + CONV_RC, :]
            cat_scr[r0:r0 + CONV_RC, ca:2 * ca] = (u_scr[r0:r0 + CONV_RC, 2 * ca:3 * ca] * cb).astype(BF16)

        out = _dot(cat_scr[...], wout_ref[...])
        o_ref[base:base + ts, :] = x_ref[base:base + ts, :] + _rmsnorm(out, post_ref[...])

    n_tiles = rows // ts
    project(0)
    for tile in range(n_tiles):
        if tile + 1 < n_tiles:
            project(tile + 1)
        mix(tile)

    @pl.when(t == pl.num_programs(1) - 1)
    def _():
        sa_ref[...] = apad_scr[CONV_HIST + rows - (wa_n - 1):CONV_HIST + rows, :]
        sb_ref[...] = zpad_scr[SUBLANES + rows - (wb_n - 1):SUBLANES + rows, :]

    apad_scr[0:CONV_HIST, :] = apad_scr[rows:rows + CONV_HIST, :]
    zpad_scr[0:SUBLANES, :] = zpad_scr[rows:rows + SUBLANES, :]


def _conv_prompt(x, i, layer, batch, pre, post, win, wa, ba, lng, lnb, wb, wout):
    T, D = x.shape
    S = T // batch
    ts = CONV_TS
    rows = CONV_TILES_PER_STEP * ts
    nt = S // rows
    ca = wa.shape[-1]
    wa_n, wb_n = wa.shape[1], wb.shape[1]
    assert wa_n - 1 <= CONV_HIST and wb_n - 1 <= SUBLANES
    gain_spec = pl.BlockSpec((None, 1, D), lambda b, t: (layer, 0, 0))
    vec_spec = pl.BlockSpec((None, 1, ca), lambda b, t: (i, 0, 0))
    return pl.pallas_call(
        _conv_prompt_body,
        grid=(batch, nt),
        in_specs=[
            pl.BlockSpec((rows, D), lambda b, t: (b * nt + t, 0)),
            gain_spec, gain_spec,
            pl.BlockSpec((None, D, 5 * ca), lambda b, t: (i, 0, 0)),
            pl.BlockSpec((None, wa_n, SUBLANES, ca), lambda b, t: (i, 0, 0, 0)),
            vec_spec, vec_spec, vec_spec,
            pl.BlockSpec((None, wb_n, ca), lambda b, t: (i, 0, 0)),
            pl.BlockSpec((None, 2 * ca, D), lambda b, t: (i, 0, 0)),
        ],
        out_specs=[
            pl.BlockSpec((rows, D), lambda b, t: (b * nt + t, 0)),
            pl.BlockSpec((None, wa_n - 1, ca), lambda b, t: (b, 0, 0)),
            pl.BlockSpec((None, wb_n - 1, ca), lambda b, t: (b, 0, 0)),
        ],
        out_shape=[
            jax.ShapeDtypeStruct((T, D), F32),
            jax.ShapeDtypeStruct((batch, wa_n - 1, ca), F32),
            jax.ShapeDtypeStruct((batch, wb_n - 1, ca), F32),
        ],
        scratch_shapes=[
            pltpu.VMEM((ts, 5 * ca), F32),
            pltpu.VMEM((ts, 5 * ca), F32),
            pltpu.VMEM((CONV_HIST + rows, ca), F32),
            pltpu.VMEM((SUBLANES - 1, CONV_HIST + ts - SUBLANES, ca), F32),
            pltpu.VMEM((SUBLANES + rows, ca), F32),
            pltpu.VMEM((ts, 2 * ca), BF16),
        ],
        compiler_params=_params("parallel", "arbitrary"),
        name="conv_prompt",
    )(x, pre, post, win, wa, ba, lng, lnb, wb, wout)


def _conv_sample_body(x_ref, sa_ref, sb_ref, pre_ref, post_ref, win_ref, wa_ref, ba_ref, lng_ref, lnb_ref,
                      wb_ref, wout_ref, o_ref, nsa_ref, nsb_ref, u_scr, glu_scr, z_scr, cat_scr):
    tt, nb, d = x_ref.shape
    ca = glu_scr.shape[-1]
    wa_n = wa_ref.shape[0]
    wb_n = wb_ref.shape[0]
    x = x_ref[...].reshape(tt * nb, d)
    h = _rmsnorm(x, pre_ref[...]).astype(BF16)
    u_scr[...] = _dot(h, win_ref[...])
    for t in range(tt):
        rows = slice(t * nb, (t + 1) * nb)
        glu_scr[t] = u_scr[rows, 0:ca] * jax.nn.sigmoid(u_scr[rows, ca:2 * ca])
        z_scr[t] = u_scr[rows, 3 * ca:4 * ca] * u_scr[rows, 4 * ca:5 * ca]

    def apad(j, rs):
        return sa_ref[j, rs, :] if j < wa_n - 1 else glu_scr[j - (wa_n - 1), rs, :]

    def zpad(j, rs):
        return sb_ref[j, rs, :] if j < wb_n - 1 else z_scr[j - (wb_n - 1), rs, :]

    for t in range(tt):
        for r in range(nb // CONV_RC):
            rs = slice(r * CONV_RC, (r + 1) * CONV_RC)
            orows = slice(t * nb + r * CONV_RC, t * nb + (r + 1) * CONV_RC)
            acc = jnp.broadcast_to(ba_ref[...], (CONV_RC, ca))
            for k in range(wa_n):
                acc = acc + wa_ref[k:k + 1, :] * apad(t + k, rs)
            a = _silu(_layernorm(acc, lng_ref[...], lnb_ref[...]))
            cat_scr[orows, 0:ca] = a.astype(BF16)
            cb = wb_ref[0:1, :] * zpad(t, rs)
            for k in range(1, wb_n):
                cb = cb + wb_ref[k:k + 1, :] * zpad(t + k, rs)
            cat_scr[orows, ca:2 * ca] = (u_scr[orows, 2 * ca:3 * ca] * cb).astype(BF16)

    out = _dot(cat_scr[...], wout_ref[...])
    o_ref[...] = (x + _rmsnorm(out, post_ref[...])).reshape(tt, nb, d)
    full = slice(0, nb)
    for j in range(wa_n - 1):
        nsa_ref[j] = apad(j + tt, full)
    for j in range(wb_n - 1):
        nsb_ref[j] = zpad(j + tt, full)


def _conv_sample(x, sa, sb, i, layer, pre, post, win, wa, ba, lng, lnb, wb, wout):
    tt, B, D = x.shape
    nb = CONV_SAMPLE_NB
    ca = wa.shape[-1]
    wa_n, wb_n = wa.shape[1], wb.shape[1]
    gain_spec = pl.BlockSpec((None, 1, D), lambda g: (layer, 0, 0))
    vec_spec = pl.BlockSpec((None, 1, ca), lambda g: (i, 0, 0))
    return pl.pallas_call(
        _conv_sample_body,
        grid=(B // nb,),
        in_specs=[
            pl.BlockSpec((tt, nb, D), lambda g: (0, g, 0)),
            pl.BlockSpec((None, wa_n - 1, nb, ca), lambda g: (i, 0, g, 0)),
            pl.BlockSpec((None, wb_n - 1, nb, ca), lambda g: (i, 0, g, 0)),
            gain_spec, gain_spec,
            pl.BlockSpec((None, D, 5 * ca), lambda g: (i, 0, 0)),
            pl.BlockSpec((None, wa_n, ca), lambda g: (i, 0, 0)),
            vec_spec, vec_spec, vec_spec,
            pl.BlockSpec((None, wb_n, ca), lambda g: (i, 0, 0)),
            pl.BlockSpec((None, 2 * ca, D), lambda g: (i, 0, 0)),
        ],
        out_specs=[
            pl.BlockSpec((tt, nb, D), lambda g: (0, g, 0)),
            pl.BlockSpec((wa_n - 1, nb, ca), lambda g: (0, g, 0)),
            pl.BlockSpec((wb_n - 1, nb, ca), lambda g: (0, g, 0)),
        ],
        out_shape=[
            jax.ShapeDtypeStruct((tt, B, D), F32),
            jax.ShapeDtypeStruct((wa_n - 1, B, ca), F32),
            jax.ShapeDtypeStruct((wb_n - 1, B, ca), F32),
        ],
        scratch_shapes=[
            pltpu.VMEM((tt * nb, 5 * ca), F32),
            pltpu.VMEM((tt, nb, ca), F32),
            pltpu.VMEM((tt, nb, ca), F32),
            pltpu.VMEM((tt * nb, 2 * ca), BF16),
        ],
        compiler_params=_params("parallel"),
        name="conv_sample",
    )(x, sa, sb, pre, post, win, wa, ba, lng, lnb, wb, wout)


def _mla_proj_body(x_ref, pre_ref, win_ref, qg_ref, wuqn_ref, wuqr_ref, kvg_ref, wukt_ref,
                   cosq_ref, sinq_ref, cosk_ref, sink_ref,
                   qlat_ref, qrope_ref, ckv_ref, kr_ref, ckvb_ref, krb_ref):
    n_heads, _, kvl = qlat_ref.shape
    rope = qrope_ref.shape[-1]
    half = rope // 2
    ql = qg_ref.shape[-1]
    nope = wuqn_ref.shape[-1] // n_heads

    width = n_heads * rope
    sub = x_ref.shape[0] // PROJ_SUB_TILES

    def stage1(r):
        rs = slice(r * sub, (r + 1) * sub)
        h = _rmsnorm(x_ref[rs, :], pre_ref[...]).astype(BF16)
        return _dot(h, win_ref[...])

    def stage2(u):
        qn = _rmsnorm(u[:, :ql], qg_ref[...]).astype(BF16)
        return _dot(qn, wuqn_ref[...]), _dot(qn, wuqr_ref[...])

    def stage3(r, u, q_nope, q_r):
        rs = slice(r * sub, (r + 1) * sub)
        ckv, kr = u[:, ql:ql + kvl], u[:, ql + kvl:]
        lane = lax.broadcasted_iota(jnp.int32, q_r.shape, 1)
        swapped = jnp.where(lane % rope < half,
                            pltpu.roll(q_r, width - half, 1), pltpu.roll(q_r, half, 1))
        q_rot = q_r * cosq_ref[rs, :] + swapped * sinq_ref[rs, :]
        for hd in range(n_heads):
            q_h = q_nope[:, hd * nope:(hd + 1) * nope].astype(BF16)
            qlat_ref[hd, rs, :] = _dot(q_h, wukt_ref[hd]).astype(BF16)
            qrope_ref[hd, rs, :] = q_rot[:, hd * rope:(hd + 1) * rope].astype(BF16)
        c = _rmsnorm(ckv, kvg_ref[...])
        ckv_ref[rs, :] = c
        ckvb_ref[rs, :] = c.astype(BF16)
        kr_swapped = jnp.concatenate([kr[:, half:], kr[:, :half]], axis=-1)
        kr_rot = kr * cosk_ref[rs, :] + kr_swapped * sink_ref[rs, :]
        kr_ref[rs, :] = kr_rot
        krb_ref[rs, :] = kr_rot.astype(BF16)

    us = [stage1(r) for r in range(PROJ_SUB_TILES)]
    qs = [stage2(u) for u in us]
    for r in range(PROJ_SUB_TILES):
        stage3(r, us[r], *qs[r])


def _mla_proj(x, j, layer, pre, win, qg, wuqn, wuqr, kvg, wukt, cosq, sinq, cosk, sink):
    T, D = x.shape
    tm = min(PROJ_TM, T)
    n_pos = cosq.shape[0] // tm
    n_heads, nope, kvl = wukt.shape[1:]
    ql = qg.shape[-1]
    rope = cosk.shape[-1]
    din = win.shape[-1]
    tok = lambda w: pl.BlockSpec((tm, w), lambda i: (i, 0))
    pos = lambda w: pl.BlockSpec((tm, w), lambda i: (i % n_pos, 0))
    return pl.pallas_call(
        _mla_proj_body,
        grid=(T // tm,),
        in_specs=[
            tok(D),
            pl.BlockSpec((None, 1, D), lambda i: (layer, 0, 0)),
            pl.BlockSpec((None, D, din), lambda i: (j, 0, 0)),
            pl.BlockSpec((None, 1, ql), lambda i: (j, 0, 0)),
            pl.BlockSpec((None, ql, n_heads * nope), lambda i: (j, 0, 0)),
            pl.BlockSpec((None, ql, n_heads * rope), lambda i: (j, 0, 0)),
            pl.BlockSpec((None, 1, kvl), lambda i: (j, 0, 0)),
            pl.BlockSpec((None, n_heads, nope, kvl), lambda i: (j, 0, 0, 0)),
            pos(n_heads * rope), pos(n_heads * rope), pos(rope), pos(rope),
        ],
        out_specs=[
            pl.BlockSpec((n_heads, tm, kvl), lambda i: (0, i, 0)),
            pl.BlockSpec((n_heads, tm, rope), lambda i: (0, i, 0)),
            tok(kvl), tok(rope), tok(kvl), tok(rope),
        ],
        out_shape=[
            jax.ShapeDtypeStruct((n_heads, T, kvl), BF16),
            jax.ShapeDtypeStruct((n_heads, T, rope), BF16),
            jax.ShapeDtypeStruct((T, kvl), F32),
            jax.ShapeDtypeStruct((T, rope), F32),
            jax.ShapeDtypeStruct((T, kvl), BF16),
            jax.ShapeDtypeStruct((T, rope), BF16),
        ],
        compiler_params=_params("parallel"),
        name="mla_proj",
    )(x, pre, win, qg, wuqn, wuqr, kvg, wukt, cosq, sinq, cosk, sink)


def _mla_out(o_heads, x, post, wuv_ref, wo_ref):
    cat = jnp.concatenate(
        [_dot(o.astype(BF16), wuv_ref[hd]).astype(BF16) for hd, o in enumerate(o_heads)], axis=-1)
    return x + _rmsnorm(_dot(cat, wo_ref[...]), post)


def _attn_prompt_body(sm_scale, ql_ref, qr_ref, kc_ref, kr_ref, x_ref, post_ref, wuv_ref, wo_ref,
                      o_ref, m_scr, l_scr, acc_scr):
    qi = pl.program_id(1)
    n_heads, tq, kvl = ql_ref.shape
    rows = n_heads * tq
    lanes = m_scr.shape[-1]
    c_exp = sm_scale * LOG2_E
    ql = ql_ref[...].reshape(rows, kvl)
    qr = qr_ref[...].reshape(rows, qr_ref.shape[-1])

    def lane_groups(a):
        return [a[:, i * lanes:(i + 1) * lanes] for i in range(tq // lanes)]

    def scores(start, diagonal):
        s = _dot_nt(ql, kc_ref[pl.ds(start, tq), :]) + _dot_nt(qr, kr_ref[pl.ds(start, tq), :])
        if diagonal:
            q_pos = lax.broadcasted_iota(jnp.int32, s.shape, 0) % tq
            k_pos = lax.broadcasted_iota(jnp.int32, s.shape, 1)
            s = jnp.where(k_pos <= q_pos, s, -jnp.inf)
        return lane_groups(s)

    def update(s_parts, start, first):
        m_cur = jnp.max(functools.reduce(jnp.maximum, s_parts), axis=-1, keepdims=True)
        if first:
            m_new = jnp.broadcast_to(m_cur, (rows, lanes))
        else:
            m_prev = m_scr[...]
            m_new = jnp.maximum(m_prev, m_cur)
            alpha = jnp.exp2((m_prev - m_new) * c_exp)
        p_parts = [jnp.exp2((sp - m_new) * c_exp) for sp in s_parts]
        p = jnp.concatenate([pp.astype(BF16) for pp in p_parts], axis=-1)
        pv = _dot(p, kc_ref[pl.ds(start, tq), :])
        if first:
            l_scr[...] = functools.reduce(jnp.add, p_parts)
            acc_scr[...] = pv
        else:
            l_scr[...] = alpha * l_scr[...] + functools.reduce(jnp.add, p_parts)
            acc_scr[...] = jnp.concatenate([alpha] * (kvl // lanes), axis=-1) * acc_scr[...] + pv
        m_scr[...] = m_new

    diag_start = pl.multiple_of(qi * tq, tq)
    update(scores(diag_start, True), diag_start, True)

    def full_pair(j, carry):
        start_a = pl.multiple_of(2 * j * tq, tq)
        start_b = pl.multiple_of((2 * j + 1) * tq, tq)
        s_a = scores(start_a, False)
        s_b = scores(start_b, False)
        update(s_a, start_a, False)
        update(s_b, start_b, False)
        return carry

    lax.fori_loop(0, qi // 2, full_pair, 0)

    @pl.when(qi % 2 == 1)
    def _():
        start = pl.multiple_of((qi - 1) * tq, tq)
        update(scores(start, False), start, False)


    o_heads = []
    for hd in range(n_heads):
        rs = slice(hd * tq, (hd + 1) * tq)
        o_heads.append(acc_scr[rs, :] / jnp.sum(l_scr[rs, :], axis=-1, keepdims=True))
    o_ref[...] = _mla_out(o_heads, x_ref[...], post_ref[...], wuv_ref, wo_ref)


def _attn_prompt(ql, qr, kc, kr, x, j, layer, batch, post, wuv, wo, sm_scale):
    n_heads, T, kvl = ql.shape
    rope = qr.shape[-1]
    D = x.shape[-1]
    S = T // batch
    tq = ATTN_TQ
    nq = S // tq
    vd = wuv.shape[-1]
    rows = n_heads * tq
    return pl.pallas_call(
        functools.partial(_attn_prompt_body, sm_scale),
        grid=(batch, nq),
        in_specs=[
            pl.BlockSpec((n_heads, tq, kvl), lambda b, q: (0, b * nq + q, 0)),
            pl.BlockSpec((n_heads, tq, rope), lambda b, q: (0, b * nq + q, 0)),
            pl.BlockSpec((S, kvl), lambda b, q: (b, 0)),
            pl.BlockSpec((S, rope), lambda b, q: (b, 0)),
            pl.BlockSpec((tq, D), lambda b, q: (b * nq + q, 0)),
            pl.BlockSpec((None, 1, D), lambda b, q: (layer, 0, 0)),
            pl.BlockSpec((None, n_heads, kvl, vd), lambda b, q: (j, 0, 0, 0)),
            pl.BlockSpec((None, n_heads * vd, D), lambda b, q: (j, 0, 0)),
        ],
        out_specs=pl.BlockSpec((tq, D), lambda b, q: (b * nq + q, 0)),
        out_shape=jax.ShapeDtypeStruct((T, D), F32),
        scratch_shapes=[
            pltpu.VMEM((rows, LANES), F32),
            pltpu.VMEM((rows, LANES), F32),
            pltpu.VMEM((rows, kvl), F32),
        ],
        compiler_params=_params("parallel", "arbitrary"),
        name="attn_prompt",
    )(ql, qr, kc, kr, x, post, wuv, wo)


def _attn_sample_body(sm_scale, layer_idx, pt_ref, ql_ref, qr_ref, cn_ref, rn_ref, kv_hbm, krt_hbm,
                      o_ref, kv_buf, krt_buf, kc_scr, krt_scr, sem):
    b = pl.program_id(0)
    n_seq = pl.num_programs(0)
    n_pages, page = kv_buf.shape[1:3]
    slot = b % 2

    def page_copies(pid, buf_slot, k):
        return (pltpu.make_async_copy(kv_hbm.at[layer_idx, pid], kv_buf.at[buf_slot, k], sem.at[0, buf_slot]),
                pltpu.make_async_copy(krt_hbm.at[layer_idx, pid], krt_buf.at[buf_slot, k], sem.at[1, buf_slot]))

    def start_fetch(seq, buf_slot):
        for k in range(n_pages):
            for copy in page_copies(pt_ref[seq * n_pages + k], buf_slot, k):
                copy.start()

    @pl.when(b == 0)
    def _():
        start_fetch(0, 0)

    @pl.when(b + 1 < n_seq)
    def _():
        start_fetch(b + 1, 1 - slot)

    pltpu.make_async_copy(kv_buf.at[slot], kv_buf.at[slot], sem.at[0, slot]).wait()
    pltpu.make_async_copy(krt_buf.at[slot], krt_buf.at[slot], sem.at[1, slot]).wait()

    for k in range(n_pages):
        kc_scr[k * page:(k + 1) * page, :] = kv_buf[slot, k].astype(BF16)
        krt_scr[:, k * page:(k + 1) * page] = krt_buf[slot, k].astype(BF16)

    ql = ql_ref[...]
    qr = qr_ref[...]
    c_exp = sm_scale * LOG2_E

    def tree(op, parts):
        while len(parts) > 1:
            parts = [op(parts[i], parts[i + 1]) for i in range(0, len(parts) - 1, 2)] + parts[len(parts) & ~1:]
        return parts[0]

    def scores(lo, hi):
        s = _dot_nt(ql, kc_scr[lo:hi, :]) + _dot(qr, krt_scr[:, lo:hi])
        return [s[:, i * LANES:(i + 1) * LANES] for i in range(s.shape[-1] // LANES)]

    def weights(s_parts):
        m = jnp.max(tree(jnp.maximum, s_parts), axis=-1, keepdims=True)
        p_parts = [jnp.exp2((sp - m) * c_exp) for sp in s_parts]
        l = jnp.sum(tree(jnp.add, p_parts), axis=-1, keepdims=True)
        return m, l, jnp.concatenate([pp.astype(BF16) for pp in p_parts], axis=-1)

    n_keys = kc_scr.shape[0]
    bounds = [(i * n_keys // SAMPLE_KEY_CHUNKS, (i + 1) * n_keys // SAMPLE_KEY_CHUNKS)
              for i in range(SAMPLE_KEY_CHUNKS)]
    chunks = []
    s_next = scores(*bounds[0])
    for i, (lo, hi) in enumerate(bounds):
        s_cur = s_next
        if i + 1 < SAMPLE_KEY_CHUNKS:
            s_next = scores(*bounds[i + 1])
        m_c, l_c, p_c = weights(s_cur)
        chunks.append((m_c, l_c, _dot(p_c, kc_scr[lo:hi, :])))

    tt = cn_ref.shape[0]
    qlf = ql.astype(F32)
    qrf = qr.astype(F32)
    q_t = lax.broadcasted_iota(jnp.int32, (ql.shape[0], 1), 0) % tt
    c_new = [cn_ref[t:t + 1, :].astype(BF16).astype(F32) for t in range(tt)]
    r_new = [rn_ref[t:t + 1, :].astype(BF16).astype(F32) for t in range(tt)]
    s_new = [jnp.sum(qlf * c_new[t], axis=-1, keepdims=True) + jnp.sum(qrf * r_new[t], axis=-1, keepdims=True)
             for t in range(tt)]

    m = tree(jnp.maximum, [c[0] for c in chunks])
    for t in range(tt):
        m = jnp.where(q_t >= t, jnp.maximum(m, s_new[t]), m)
    l = jnp.zeros_like(m)
    acc = jnp.zeros_like(chunks[0][2])
    for m_c, l_c, acc_c in chunks:
        w = jnp.exp2((m_c - m) * c_exp)
        l = l + w * l_c
        acc = acc + w * acc_c
    for t in range(tt):
        p_t = jnp.where(q_t >= t, jnp.exp2((s_new[t] - m) * c_exp), 0.0)
        l = l + p_t
        acc = acc + p_t * c_new[t]
    o_ref[...] = acc / l


def _attn_sample(ql, qr, c_new, r_new, cache_kv, cache_krt, page_table, j, sm_scale):
    B, rows, kvl = ql.shape
    rope = qr.shape[-1]
    tt = c_new.shape[1]
    page = cache_kv.shape[2]
    n_pages = page_table.shape[1]
    pt = page_table.reshape(-1)

    per_b = lambda w: pl.BlockSpec((None, rows, w), lambda b, pt_ref: (b, 0, 0))
    new_b = lambda w: pl.BlockSpec((None, tt, w), lambda b, pt_ref: (b, 0, 0))
    hbm = pl.BlockSpec(memory_space=pl.ANY)
    return pl.pallas_call(
        functools.partial(_attn_sample_body, sm_scale, j),
        grid_spec=pltpu.PrefetchScalarGridSpec(
            num_scalar_prefetch=1,
            grid=(B,),
            in_specs=[per_b(kvl), per_b(rope), new_b(kvl), new_b(rope), hbm, hbm],
            out_specs=pl.BlockSpec((None, rows, kvl), lambda b, pt_ref: (b, 0, 0)),
            scratch_shapes=[
                pltpu.VMEM((2, n_pages, page, kvl), F32),
                pltpu.VMEM((2, n_pages, rope, page), F32),
                pltpu.VMEM((n_pages * page, kvl), BF16),
                pltpu.VMEM((rope, n_pages * page), BF16),
                pltpu.SemaphoreType.DMA((2, 2)),
            ],
        ),
        out_shape=jax.ShapeDtypeStruct((B, rows, kvl), F32),
        compiler_params=_params("arbitrary"),
        name="attn_sample",
    )(pt, ql, qr, c_new, r_new, cache_kv, cache_krt)


def _mla_out_sample_body(oh_ref, x_ref, post_ref, wuv_ref, wo_ref, o_ref):
    o_heads = [oh_ref[hd] for hd in range(oh_ref.shape[0])]
    o_ref[...] = _mla_out(o_heads, x_ref[...], post_ref[...], wuv_ref, wo_ref)


def _mla_out_sample(oh, x, j, layer, post, wuv, wo):
    n_heads, T, kvl = oh.shape
    D = x.shape[-1]
    vd = wuv.shape[-1]
    return pl.pallas_call(
        _mla_out_sample_body,
        grid=(1,),
        in_specs=[
            pl.BlockSpec((n_heads, T, kvl), lambda g: (0, 0, 0)),
            pl.BlockSpec((T, D), lambda g: (0, 0)),
            pl.BlockSpec((None, 1, D), lambda g: (layer, 0, 0)),
            pl.BlockSpec((None, n_heads, kvl, vd), lambda g: (j, 0, 0, 0)),
            pl.BlockSpec((None, n_heads * vd, D), lambda g: (j, 0, 0)),
        ],
        out_specs=pl.BlockSpec((T, D), lambda g: (0, 0)),
        out_shape=jax.ShapeDtypeStruct((T, D), F32),
        compiler_params=_params("arbitrary"),
        name="mla_out_sample",
    )(oh, x, post, wuv, wo)


def _rope_tables(pos, rope, n_heads):
    half = rope // 2
    inv_freq = 1.0 / (ROPE_THETA ** (jnp.arange(half, dtype=F32) * (2.0 / rope)))
    ang = pos.astype(F32)[:, None] * inv_freq[None, :]
    cos, sin = jnp.cos(ang), jnp.sin(ang)
    cos_k = jnp.concatenate([cos, cos], axis=-1)
    sin_k = jnp.concatenate([-sin, sin], axis=-1)
    return jnp.tile(cos_k, (1, n_heads)), jnp.tile(sin_k, (1, n_heads)), cos_k, sin_k


def kernel(x_prompt, x_sample, state_conv_a, state_conv_b, cache_kv_latent, cache_k_rope, page_table,
           mix_pre_gain, mix_post_gain, ffn_pre_gain, ffn_post_gain, w_ffn_gate, w_ffn_up, w_ffn_down,
           w_in_conv, conv_a_w, conv_a_b, ln_a_gain, ln_a_bias, conv_b_w, w_out_conv,
           w_in_mla, q_norm_gain, w_uq, kv_norm_gain, w_uk, w_uv, w_o_mla):
    Bp, Sp, D = x_prompt.shape
    Bs, Ss, _ = x_sample.shape
    depth = mix_pre_gain.shape[0]
    n_mla = w_in_mla.shape[0]
    kvl, n_heads, nope = w_uk.shape[1:]
    ql = q_norm_gain.shape[-1]
    rope = w_in_mla.shape[-1] - ql - kvl
    sm_scale = float(nope + rope) ** -0.5
    past_len = page_table.shape[1] * cache_kv_latent.shape[2]

    row3 = lambda a: a.reshape(a.shape[0], 1, a.shape[1])
    mix_pre, mix_post = row3(mix_pre_gain), row3(mix_post_gain)
    ffn_pre, ffn_post = row3(ffn_pre_gain), row3(ffn_post_gain)
    wg, wu, wd = w_ffn_gate.astype(BF16), w_ffn_up.astype(BF16), w_ffn_down.astype(BF16)
    win_c, wout_c = w_in_conv.astype(BF16), w_out_conv.astype(BF16)
    ba, lng, lnb = row3(conv_a_b), row3(ln_a_gain), row3(ln_a_bias)
    wa_tiles = jnp.broadcast_to(conv_a_w[:, :, None, :], conv_a_w.shape[:2] + (SUBLANES,) + conv_a_w.shape[2:])
    win_m = w_in_mla.astype(BF16)
    qg, kvg = row3(q_norm_gain), row3(kv_norm_gain)
    wuq4 = w_uq.reshape(n_mla, ql, n_heads, nope + rope)
    wuqn = wuq4[..., :nope].reshape(n_mla, ql, n_heads * nope).astype(BF16)
    wuqr = wuq4[..., nope:].reshape(n_mla, ql, n_heads * rope).astype(BF16)
    wukt = jnp.transpose(w_uk, (0, 2, 3, 1)).astype(BF16)
    wuv = jnp.transpose(w_uv, (0, 2, 1, 3)).astype(BF16)
    wo = w_o_mla.astype(BF16)

    tab_p = _rope_tables(jnp.arange(Sp), rope, n_heads)
    tab_s = _rope_tables(past_len + jnp.repeat(jnp.arange(Ss), Bs), rope, n_heads)

    yp = x_prompt.reshape(Bp * Sp, D)
    ys = jnp.transpose(x_sample, (1, 0, 2))
    cache_krt = jnp.swapaxes(cache_k_rope, 2, 3)
    sa_tm = jnp.transpose(state_conv_a, (0, 2, 1, 3))
    sb_tm = jnp.transpose(state_conv_b, (0, 2, 1, 3))

    ca_p, ca_s, cb_p, cb_s = [], [], [], []
    kv_p, kv_s, kr_p, kr_s = [], [], [], []
    for layer in range(depth):
        if layer % 2 == 0:
            i = layer // 2
            prm = (mix_pre, mix_post, win_c, conv_a_w, ba, lng, lnb, conv_b_w, wout_c)
            yp, na_p, nb_p = _conv_prompt(yp, i, layer, Bp, *prm[:3], wa_tiles, *prm[4:])
            ys, na_s, nb_s = _conv_sample(ys, sa_tm, sb_tm, i, layer, *prm)
            ca_p.append(na_p); cb_p.append(nb_p); ca_s.append(na_s); cb_s.append(nb_s)
        else:
            j = layer // 2
            prm = (mix_pre, win_m, qg, wuqn, wuqr, kvg, wukt)
            qlp, qrp, c_p, r_p, cb16, rb16 = _mla_proj(yp, j, layer, *prm, *tab_p)
            yp = _attn_prompt(qlp, qrp, cb16, rb16, yp, j, layer, Bp, mix_post, wuv, wo, sm_scale)
            ys2 = ys.reshape(Ss * Bs, D)
            qls, qrs, c_s, r_s, _, _ = _mla_proj(ys2, j, layer, *prm, *tab_s)
            to_b = lambda a: jnp.transpose(a.reshape(n_heads, Ss, Bs, a.shape[-1]), (2, 0, 1, 3)).reshape(
                Bs, n_heads * Ss, a.shape[-1])
            c_sb = jnp.transpose(c_s.reshape(Ss, Bs, kvl), (1, 0, 2))
            r_sb = jnp.transpose(r_s.reshape(Ss, Bs, rope), (1, 0, 2))
            o_lat = _attn_sample(to_b(qls), to_b(qrs), c_sb, r_sb, cache_kv_latent, cache_krt,
                                 page_table, j, sm_scale)
            oh = jnp.transpose(o_lat.reshape(Bs, n_heads, Ss, kvl), (1, 2, 0, 3)).reshape(n_heads, Ss * Bs, kvl)
            ys = _mla_out_sample(oh, ys2, j, layer, mix_post, wuv, wo).reshape(Ss, Bs, D)
            kv_p.append(c_p.reshape(Bp, Sp, kvl)); kr_p.append(r_p.reshape(Bp, Sp, rope))
            kv_s.append(c_sb); kr_s.append(r_sb)
        yp = _ffn(yp, layer, ffn_pre, ffn_post, wg, wu, wd)
        ys = _ffn(ys.reshape(Ss * Bs, D), layer, ffn_pre, ffn_post, wg, wu, wd).reshape(Ss, Bs, D)

    from_tm = lambda xs: jnp.transpose(jnp.stack(xs), (0, 2, 1, 3))
    return (yp.reshape(Bp, Sp, D), jnp.transpose(ys, (1, 0, 2)),
            jnp.stack(ca_p), from_tm(ca_s), jnp.stack(cb_p), from_tm(cb_s),
            jnp.stack(kv_p), jnp.stack(kv_s), jnp.stack(kr_p), jnp.stack(kr_s))
```

```python
import functools

import jax
import jax.numpy as jnp
from jax import lax
from jax.experimental import pallas as pl
from jax.experimental.pallas import tpu as pltpu

F32 = jnp.float32
BF16 = jnp.bfloat16

ROPE_THETA = 10000.0
RMS_EPS = 1e-6
LN_EPS = 1e-5
LOG2_E = 1.4426950408889634

V7X_VMEM_LIMIT_BYTES = 56 * 1024 * 1024
SUBLANES = 8
LANES = 128

FFN_TM = 1024
FFN_TF = 256
CONV_TS = 512
CONV_TILES_PER_STEP = 2
CONV_RC = 64
CONV_HIST = 32
CONV_SAMPLE_NB = 64
PROJ_TM = 1024
PROJ_SUB_TILES = 2
ATTN_TQ = 256
SAMPLE_KEY_CHUNKS = 2


def _params(*sem):
    return pltpu.CompilerParams(dimension_semantics=sem, vmem_limit_bytes=V7X_VMEM_LIMIT_BYTES)


def _rmsnorm(x, g):
    return x * lax.rsqrt(jnp.mean(x * x, axis=-1, keepdims=True) + RMS_EPS) * g


def _layernorm(x, g, b):
    mu = jnp.mean(x, axis=-1, keepdims=True)
    xc = x - mu
    var = jnp.mean(xc * xc, axis=-1, keepdims=True)
    return xc * lax.rsqrt(var + LN_EPS) * g + b


def _silu(x):
    return x * jax.nn.sigmoid(x)


def _dot(a, b):
    return jnp.dot(a, b, preferred_element_type=F32)


def _dot_nt(a, b):
    return lax.dot_general(a, b, (((1,), (1,)), ((), ())), preferred_element_type=F32)


def _ffn_body(x_ref, pre_ref, post_ref, wg_ref, wu_ref, wd_ref, o_ref, h_scr, act_scr):
    h_scr[...] = _rmsnorm(x_ref[...], pre_ref[...]).astype(BF16)
    for c in range(wg_ref.shape[-1] // FFN_TF):
        cols = slice(c * FFN_TF, (c + 1) * FFN_TF)
        gate = _dot(h_scr[...], wg_ref[:, cols])
        up = _dot(h_scr[...], wu_ref[:, cols])
        act_scr[:, cols] = (_silu(gate) * up).astype(BF16)
    o_ref[...] = x_ref[...] + _rmsnorm(_dot(act_scr[...], wd_ref[...]), post_ref[...])


def _ffn(x, layer, pre, post, wg, wu, wd):
    T, D = x.shape
    F = wg.shape[-1]
    tm = min(FFN_TM, T)
    gain_spec = pl.BlockSpec((None, 1, D), lambda i: (layer, 0, 0))
    resident = dict(pipeline_mode=pl.Buffered(1))
    return pl.pallas_call(
        _ffn_body,
        grid=(T // tm,),
        in_specs=[
            pl.BlockSpec((tm, D), lambda i: (i, 0)),
            gain_spec, gain_spec,
            pl.BlockSpec((None, D, F), lambda i: (layer, 0, 0), **resident),
            pl.BlockSpec((None, D, F), lambda i: (layer, 0, 0), **resident),
            pl.BlockSpec((None, F, D), lambda i: (layer, 0, 0), **resident),
        ],
        out_specs=pl.BlockSpec((tm, D), lambda i: (i, 0)),
        out_shape=jax.ShapeDtypeStruct((T, D), F32),
        scratch_shapes=[pltpu.VMEM((tm, D), BF16), pltpu.VMEM((tm, F), BF16)],
        compiler_params=_params("parallel"),
        name="ffn",
    )(x, pre, post, wg, wu, wd)


def _conv_prompt_body(x_ref, pre_ref, post_ref, win_ref, wa_ref, ba_ref, lng_ref, lnb_ref, wb_ref, wout_ref,
                      o_ref, sa_ref, sb_ref, u0_scr, u1_scr, apad_scr, shift_scr, zpad_scr, cat_scr):
    t = pl.program_id(1)
    rows = x_ref.shape[0]
    ts = u0_scr.shape[0]
    ca = apad_scr.shape[1]
    wa_n = wa_ref.shape[0]
    wb_n = wb_ref.shape[0]
    u_scrs = (u0_scr, u1_scr)
    a0 = CONV_HIST - (wa_n - 1)
    z0 = SUBLANES - (wb_n - 1)
    n_shift_rows = shift_scr.shape[1]

    @pl.when(t == 0)
    def _():
        apad_scr[0:CONV_HIST, :] = jnp.zeros((CONV_HIST, ca), F32)
        zpad_scr[0:SUBLANES, :] = jnp.zeros((SUBLANES, ca), F32)

    def project(tile):
        xs = x_ref[tile * ts:(tile + 1) * ts, :]
        u_scrs[tile % 2][...] = _dot(_rmsnorm(xs, pre_ref[...]).astype(BF16), win_ref[...])

    def mix(tile):
        u_scr = u_scrs[tile % 2]
        base = tile * ts
        apad_scr[CONV_HIST + base:CONV_HIST + base + ts, :] = u_scr[:, 0:ca] * jax.nn.sigmoid(u_scr[:, ca:2 * ca])
        zpad_scr[SUBLANES + base:SUBLANES + base + ts, :] = u_scr[:, 3 * ca:4 * ca] * u_scr[:, 4 * ca:5 * ca]
        for r in range(1, SUBLANES):
            shift_scr[r - 1] = apad_scr[base + r:base + r + n_shift_rows, :]

        def tap_rows(r0, k):
            off = a0 + k
            lo, r = off - off % SUBLANES, off % SUBLANES
            if r == 0:
                return apad_scr[base + r0 + lo:base + r0 + lo + CONV_RC, :]
            return shift_scr[r - 1, r0 + lo:r0 + lo + CONV_RC, :]

        for c in range(ts // CONV_RC):
            r0 = c * CONV_RC
            acc = jnp.broadcast_to(ba_ref[...], (CONV_RC, ca))
            for k in range(wa_n):
                w_rows = jnp.concatenate([wa_ref[k]] * (CONV_RC // SUBLANES), axis=0)
                acc = acc + w_rows * tap_rows(r0, k)
            a = _silu(_layernorm(acc, lng_ref[...], lnb_ref[...]))
            cat_scr[r0:r0 + CONV_RC, 0:ca] = a.astype(BF16)
            zr = base + r0 + z0
            cb = wb_ref[0:1, :] * zpad_scr[zr:zr + CONV_RC, :]
            for k in range(1, wb_n):
                cb = cb + wb_ref[k:k + 1, :] * zpad_scr[zr + k:zr + k + CONV_RC, :]
            cat_scr[r0:r0 + CONV_RC, ca:2 * ca] = (u_scr[r0:r0 + CONV_RC, 2 * ca:3 * ca] * cb).astype(BF16)

        out = _dot(cat_scr[...], wout_ref[...])
        o_ref[base:base + ts, :] = x_ref[base:base + ts, :] + _rmsnorm(out, post_ref[...])

    n_tiles = rows // ts
    project(0)
    for tile in range(n_tiles):
        if tile + 1 < n_tiles:
            project(tile + 1)
        mix(tile)

    @pl.when(t == pl.num_programs(1) - 1)
    def _():
        sa_ref[...] = apad_scr[CONV_HIST + rows - (wa_n - 1):CONV_HIST + rows, :]
        sb_ref[...] = zpad_scr[SUBLANES + rows - (wb_n - 1):SUBLANES + rows, :]

    apad_scr[0:CONV_HIST, :] = apad_scr[rows:rows + CONV_HIST, :]
    zpad_scr[0:SUBLANES, :] = zpad_scr[rows:rows + SUBLANES, :]


def _conv_prompt(x, i, layer, batch, pre, post, win, wa, ba, lng, lnb, wb, wout):
    T, D = x.shape
    S = T // batch
    ts = CONV_TS
    rows = CONV_TILES_PER_STEP * ts
    nt = S // rows
    ca = wa.shape[-1]
    wa_n, wb_n = wa.shape[1], wb.shape[1]
    assert wa_n - 1 <= CONV_HIST and wb_n - 1 <= SUBLANES
    gain_spec = pl.BlockSpec((None, 1, D), lambda b, t: (layer, 0, 0))
    vec_spec = pl.BlockSpec((None, 1, ca), lambda b, t: (i, 0, 0))
    return pl.pallas_call(
        _conv_prompt_body,
        grid=(batch, nt),
        in_specs=[
            pl.BlockSpec((rows, D), lambda b, t: (b * nt + t, 0)),
            gain_spec, gain_spec,
            pl.BlockSpec((None, D, 5 * ca), lambda b, t: (i, 0, 0)),
            pl.BlockSpec((None, wa_n, SUBLANES, ca), lambda b, t: (i, 0, 0, 0)),
            vec_spec, vec_spec, vec_spec,
            pl.BlockSpec((None, wb_n, ca), lambda b, t: (i, 0, 0)),
            pl.BlockSpec((None, 2 * ca, D), lambda b, t: (i, 0, 0)),
        ],
        out_specs=[
            pl.BlockSpec((rows, D), lambda b, t: (b * nt + t, 0)),
            pl.BlockSpec((None, wa_n - 1, ca), lambda b, t: (b, 0, 0)),
            pl.BlockSpec((None, wb_n - 1, ca), lambda b, t: (b, 0, 0)),
        ],
        out_shape=[
            jax.ShapeDtypeStruct((T, D), F32),
            jax.ShapeDtypeStruct((batch, wa_n - 1, ca), F32),
            jax.ShapeDtypeStruct((batch, wb_n - 1, ca), F32),
        ],
        scratch_shapes=[
            pltpu.VMEM((ts, 5 * ca), F32),
            pltpu.VMEM((ts, 5 * ca), F32),
            pltpu.VMEM((CONV_HIST + rows, ca), F32),
            pltpu.VMEM((SUBLANES - 1, CONV_HIST + ts - SUBLANES, ca), F32),
            pltpu.VMEM((SUBLANES + rows, ca), F32),
            pltpu.VMEM((ts, 2 * ca), BF16),
        ],
        compiler_params=_params("parallel", "arbitrary"),
        name="conv_prompt",
    )(x, pre, post, win, wa, ba, lng, lnb, wb, wout)


def _conv_sample_body(x_ref, sa_ref, sb_ref, pre_ref, post_ref, win_ref, wa_ref, ba_ref, lng_ref, lnb_ref,
                      wb_ref, wout_ref, o_ref, nsa_ref, nsb_ref, u_scr, glu_scr, z_scr, cat_scr):
    tt, nb, d = x_ref.shape
    ca = glu_scr.shape[-1]
    wa_n = wa_ref.shape[0]
    wb_n = wb_ref.shape[0]
    x = x_ref[...].reshape(tt * nb, d)
    h = _rmsnorm(x, pre_ref[...]).astype(BF16)
    u_scr[...] = _dot(h, win_ref[...])
    for t in range(tt):
        rows = slice(t * nb, (t + 1) * nb)
        glu_scr[t] = u_scr[rows, 0:ca] * jax.nn.sigmoid(u_scr[rows, ca:2 * ca])
        z_scr[t] = u_scr[rows, 3 * ca:4 * ca] * u_scr[rows, 4 * ca:5 * ca]

    def apad(j, rs):
        return sa_ref[j, rs, :] if j < wa_n - 1 else glu_scr[j - (wa_n - 1), rs, :]

    def zpad(j, rs):
        return sb_ref[j, rs, :] if j < wb_n - 1 else z_scr[j - (wb_n - 1), rs, :]

    for t in range(tt):
        for r in range(nb // CONV_RC):
            rs = slice(r * CONV_RC, (r + 1) * CONV_RC)
            orows = slice(t * nb + r * CONV_RC, t * nb + (r + 1) * CONV_RC)
            acc = jnp.broadcast_to(ba_ref[...], (CONV_RC, ca))
            for k in range(wa_n):
                acc = acc + wa_ref[k:k + 1, :] * apad(t + k, rs)
            a = _silu(_layernorm(acc, lng_ref[...], lnb_ref[...]))
            cat_scr[orows, 0:ca] = a.astype(BF16)
            cb = wb_ref[0:1, :] * zpad(t, rs)
            for k in range(1, wb_n):
                cb = cb + wb_ref[k:k + 1, :] * zpad(t + k, rs)
            cat_scr[orows, ca:2 * ca] = (u_scr[orows, 2 * ca:3 * ca] * cb).astype(BF16)

    out = _dot(cat_scr[...], wout_ref[...])
    o_ref[...] = (x + _rmsnorm(out, post_ref[...])).reshape(tt, nb, d)
    full = slice(0, nb)
    for j in range(wa_n - 1):
        nsa_ref[j] = apad(j + tt, full)
    for j in range(wb_n - 1):
        nsb_ref[j] = zpad(j + tt, full)


def _conv_sample(x, sa, sb, i, layer, pre, post, win, wa, ba, lng, lnb, wb, wout):
    tt, B, D = x.shape
    nb = CONV_SAMPLE_NB
    ca = wa.shape[-1]
    wa_n, wb_n = wa.shape[1], wb.shape[1]
    gain_spec = pl.BlockSpec((None, 1, D), lambda g: (layer, 0, 0))
    vec_spec = pl.BlockSpec((None, 1, ca), lambda g: (i, 0, 0))
    return pl.pallas_call(
        _conv_sample_body,
        grid=(B // nb,),
        in_specs=[
            pl.BlockSpec((tt, nb, D), lambda g: (0, g, 0)),
            pl.BlockSpec((None, wa_n - 1, nb, ca), lambda g: (i, 0, g, 0)),
            pl.BlockSpec((None, wb_n - 1, nb, ca), lambda g: (i, 0, g, 0)),
            gain_spec, gain_spec,
            pl.BlockSpec((None, D, 5 * ca), lambda g: (i, 0, 0)),
            pl.BlockSpec((None, wa_n, ca), lambda g: (i, 0, 0)),
            vec_spec, vec_spec, vec_spec,
            pl.BlockSpec((None, wb_n, ca), lambda g: (i, 0, 0)),
            pl.BlockSpec((None, 2 * ca, D), lambda g: (i, 0, 0)),
        ],
        out_specs=[
            pl.BlockSpec((tt, nb, D), lambda g: (0, g, 0)),
            pl.BlockSpec((wa_n - 1, nb, ca), lambda g: (0, g, 0)),
            pl.BlockSpec((wb_n - 1, nb, ca), lambda g: (0, g, 0)),
        ],
        out_shape=[
            jax.ShapeDtypeStruct((tt, B, D), F32),
            jax.ShapeDtypeStruct((wa_n - 1, B, ca), F32),
            jax.ShapeDtypeStruct((wb_n - 1, B, ca), F32),
        ],
        scratch_shapes=[
            pltpu.VMEM((tt * nb, 5 * ca), F32),
            pltpu.VMEM((tt, nb, ca), F32),
            pltpu.VMEM((tt, nb, ca), F32),
            pltpu.VMEM((tt * nb, 2 * ca), BF16),
        ],
        compiler_params=_params("parallel"),
        name="conv_sample",
    )(x, sa, sb, pre, post, win, wa, ba, lng, lnb, wb, wout)


def _mla_proj_body(x_ref, pre_ref, win_ref, qg_ref, wuqn_ref, wuqr_ref, kvg_ref, wukt_ref,
                   cosq_ref, sinq_ref, cosk_ref, sink_ref,
                   qlat_ref, qrope_ref, ckv_ref, kr_ref, ckvb_ref, krb_ref):
    n_heads, _, kvl = qlat_ref.shape
    rope = qrope_ref.shape[-1]
    half = rope // 2
    ql = qg_ref.shape[-1]
    nope = wuqn_ref.shape[-1] // n_heads

    width = n_heads * rope
    sub = x_ref.shape[0] // PROJ_SUB_TILES

    def stage1(r):
        rs = slice(r * sub, (r + 1) * sub)
        h = _rmsnorm(x_ref[rs, :], pre_ref[...]).astype(BF16)
        return _dot(h, win_ref[...])

    def stage2(u):
        qn = _rmsnorm(u[:, :ql], qg_ref[...]).astype(BF16)
        return _dot(qn, wuqn_ref[...]), _dot(qn, wuqr_ref[...])

    def stage3(r, u, q_nope, q_r):
        rs = slice(r * sub, (r + 1) * sub)
        ckv, kr = u[:, ql:ql + kvl], u[:, ql + kvl:]
        lane = lax.broadcasted_iota(jnp.int32, q_r.shape, 1)
        swapped = jnp.where(lane % rope < half,
                            pltpu.roll(q_r, width - half, 1), pltpu.roll(q_r, half, 1))
        q_rot = q_r * cosq_ref[rs, :] + swapped * sinq_ref[rs, :]
        for hd in range(n_heads):
            q_h = q_nope[:, hd * nope:(hd + 1) * nope].astype(BF16)
            qlat_ref[hd, rs, :] = _dot(q_h, wukt_ref[hd]).astype(BF16)
            qrope_ref[hd, rs, :] = q_rot[:, hd * rope:(hd + 1) * rope].astype(BF16)
        c = _rmsnorm(ckv, kvg_ref[...])
        ckv_ref[rs, :] = c
        ckvb_ref[rs, :] = c.astype(BF16)
        kr_swapped = jnp.concatenate([kr[:, half:], kr[:, :half]], axis=-1)
        kr_rot = kr * cosk_ref[rs, :] + kr_swapped * sink_ref[rs, :]
        kr_ref[rs, :] = kr_rot
        krb_ref[rs, :] = kr_rot.astype(BF16)

    us = [stage1(r) for r in range(PROJ_SUB_TILES)]
    qs = [stage2(u) for u in us]
    for r in range(PROJ_SUB_TILES):
        stage3(r, us[r], *qs[r])


def _mla_proj(x, j, layer, pre, win, qg, wuqn, wuqr, kvg, wukt, cosq, sinq, cosk, sink):
    T, D = x.shape
    tm = min(PROJ_TM, T)
    n_pos = cosq.shape[0] // tm
    n_heads, nope, kvl = wukt.shape[1:]
    ql = qg.shape[-1]
    rope = cosk.shape[-1]
    din = win.shape[-1]
    tok = lambda w: pl.BlockSpec((tm, w), lambda i: (i, 0))
    pos = lambda w: pl.BlockSpec((tm, w), lambda i: (i % n_pos, 0))
    return pl.pallas_call(
        _mla_proj_body,
        grid=(T // tm,),
        in_specs=[
            tok(D),
            pl.BlockSpec((None, 1, D), lambda i: (layer, 0, 0)),
            pl.BlockSpec((None, D, din), lambda i: (j, 0, 0)),
            pl.BlockSpec((None, 1, ql), lambda i: (j, 0, 0)),
            pl.BlockSpec((None, ql, n_heads * nope), lambda i: (j, 0, 0)),
            pl.BlockSpec((None, ql, n_heads * rope), lambda i: (j, 0, 0)),
            pl.BlockSpec((None, 1, kvl), lambda i: (j, 0, 0)),
            pl.BlockSpec((None, n_heads, nope, kvl), lambda i: (j, 0, 0, 0)),
            pos(n_heads * rope), pos(n_heads * rope), pos(rope), pos(rope),
        ],
        out_specs=[
            pl.BlockSpec((n_heads, tm, kvl), lambda i: (0, i, 0)),
            pl.BlockSpec((n_heads, tm, rope), lambda i: (0, i, 0)),
            tok(kvl), tok(rope), tok(kvl), tok(rope),
        ],
        out_shape=[
            jax.ShapeDtypeStruct((n_heads, T, kvl), BF16),
            jax.ShapeDtypeStruct((n_heads, T, rope), BF16),
            jax.ShapeDtypeStruct((T, kvl), F32),
            jax.ShapeDtypeStruct((T, rope), F32),
            jax.ShapeDtypeStruct((T, kvl), BF16),
            jax.ShapeDtypeStruct((T, rope), BF16),
        ],
        compiler_params=_params("parallel"),
        name="mla_proj",
    )(x, pre, win, qg, wuqn, wuqr, kvg, wukt, cosq, sinq, cosk, sink)


def _mla_out(o_heads, x, post, wuv_ref, wo_ref):
    cat = jnp.concatenate(
        [_dot(o.astype(BF16), wuv_ref[hd]).astype(BF16) for hd, o in enumerate(o_heads)], axis=-1)
    return x + _rmsnorm(_dot(cat, wo_ref[...]), post)


def _attn_prompt_body(sm_scale, ql_ref, qr_ref, kc_ref, kr_ref, x_ref, post_ref, wuv_ref, wo_ref,
                      o_ref, m_scr, l_scr, acc_scr):
    qi = pl.program_id(1)
    n_heads, tq, kvl = ql_ref.shape
    rows = n_heads * tq
    lanes = m_scr.shape[-1]
    c_exp = sm_scale * LOG2_E
    ql = ql_ref[...].reshape(rows, kvl)
    qr = qr_ref[...].reshape(rows, qr_ref.shape[-1])

    def lane_groups(a):
        return [a[:, i * lanes:(i + 1) * lanes] for i in range(tq // lanes)]

    def scores(start, diagonal):
        s = _dot_nt(ql, kc_ref[pl.ds(start, tq), :]) + _dot_nt(qr, kr_ref[pl.ds(start, tq), :])
        if diagonal:
            q_pos = lax.broadcasted_iota(jnp.int32, s.shape, 0) % tq
            k_pos = lax.broadcasted_iota(jnp.int32, s.shape, 1)
            s = jnp.where(k_pos <= q_pos, s, -jnp.inf)
        return lane_groups(s)

    def update(s_parts, start, first):
        m_cur = jnp.max(functools.reduce(jnp.maximum, s_parts), axis=-1, keepdims=True)
        if first:
            m_new = jnp.broadcast_to(m_cur, (rows, lanes))
        else:
            m_prev = m_scr[...]
            m_new = jnp.maximum(m_prev, m_cur)
            alpha = jnp.exp2((m_prev - m_new) * c_exp)
        p_parts = [jnp.exp2((sp - m_new) * c_exp) for sp in s_parts]
        p = jnp.concatenate([pp.astype(BF16) for pp in p_parts], axis=-1)
        pv = _dot(p, kc_ref[pl.ds(start, tq), :])
        if first:
            l_scr[...] = functools.reduce(jnp.add, p_parts)
            acc_scr[...] = pv
        else:
            l_scr[...] = alpha * l_scr[...] + functools.reduce(jnp.add, p_parts)
            acc_scr[...] = jnp.concatenate([alpha] * (kvl // lanes), axis=-1) * acc_scr[...] + pv
        m_scr[...] = m_new

    diag_start = pl.multiple_of(qi * tq, tq)
    update(scores(diag_start, True), diag_start, True)

    def run_blocks(first_block, count):
        starts = [pl.multiple_of((first_block + i) * tq, tq) for i in range(count)]
        s_next = scores(starts[0], False)
        for i in range(count):
            s_cur = s_next
            if i + 1 < count:
                s_next = scores(starts[i + 1], False)
            update(s_cur, starts[i], False)

    def full_quad(j, carry):
        run_blocks(4 * j, 4)
        return carry

    lax.fori_loop(0, qi // 4, full_quad, 0)

    @pl.when(qi % 4 >= 2)
    def _():
        run_blocks((qi // 4) * 4, 2)

    @pl.when(qi % 2 == 1)
    def _():
        run_blocks(qi - 1, 1)


    o_heads = []
    for hd in range(n_heads):
        rs = slice(hd * tq, (hd + 1) * tq)
        o_heads.append(acc_scr[rs, :] / jnp.sum(l_scr[rs, :], axis=-1, keepdims=True))
    o_ref[...] = _mla_out(o_heads, x_ref[...], post_ref[...], wuv_ref, wo_ref)


def _attn_prompt(ql, qr, kc, kr, x, j, layer, batch, post, wuv, wo, sm_scale):
    n_heads, T, kvl = ql.shape
    rope = qr.shape[-1]
    D = x.shape[-1]
    S = T // batch
    tq = ATTN_TQ
    nq = S // tq
    vd = wuv.shape[-1]
    rows = n_heads * tq
    return pl.pallas_call(
        functools.partial(_attn_prompt_body, sm_scale),
        grid=(batch, nq),
        in_specs=[
            pl.BlockSpec((n_heads, tq, kvl), lambda b, q: (0, b * nq + q, 0)),
            pl.BlockSpec((n_heads, tq, rope), lambda b, q: (0, b * nq + q, 0)),
            pl.BlockSpec((S, kvl), lambda b, q: (b, 0)),
            pl.BlockSpec((S, rope), lambda b, q: (b, 0)),
            pl.BlockSpec((tq, D), lambda b, q: (b * nq + q, 0)),
            pl.BlockSpec((None, 1, D), lambda b, q: (layer, 0, 0)),
            pl.BlockSpec((None, n_heads, kvl, vd), lambda b, q: (j, 0, 0, 0)),
            pl.BlockSpec((None, n_heads * vd, D), lambda b, q: (j, 0, 0)),
        ],
        out_specs=pl.BlockSpec((tq, D), lambda b, q: (b * nq + q, 0)),
        out_shape=jax.ShapeDtypeStruct((T, D), F32),
        scratch_shapes=[
            pltpu.VMEM((rows, LANES), F32),
            pltpu.VMEM((rows, LANES), F32),
            pltpu.VMEM((rows, kvl), F32),
        ],
        compiler_params=_params("parallel", "arbitrary"),
        name="attn_prompt",
    )(ql, qr, kc, kr, x, post, wuv, wo)


def _attn_sample_body(sm_scale, layer_idx, pt_ref, ql_ref, qr_ref, cn_ref, rn_ref, kv_hbm, krt_hbm,
                      o_ref, kv_buf, krt_buf, kc_scr, krt_scr, sem):
    b = pl.program_id(0)
    n_seq = pl.num_programs(0)
    n_pages, page = kv_buf.shape[1:3]
    slot = b % 2

    def page_copies(pid, buf_slot, k):
        return (pltpu.make_async_copy(kv_hbm.at[layer_idx, pid], kv_buf.at[buf_slot, k], sem.at[0, buf_slot]),
                pltpu.make_async_copy(krt_hbm.at[layer_idx, pid], krt_buf.at[buf_slot, k], sem.at[1, buf_slot]))

    def start_fetch(seq, buf_slot):
        for k in range(n_pages):
            for copy in page_copies(pt_ref[seq * n_pages + k], buf_slot, k):
                copy.start()

    @pl.when(b == 0)
    def _():
        start_fetch(0, 0)

    @pl.when(b + 1 < n_seq)
    def _():
        start_fetch(b + 1, 1 - slot)

    pltpu.make_async_copy(kv_buf.at[slot], kv_buf.at[slot], sem.at[0, slot]).wait()
    pltpu.make_async_copy(krt_buf.at[slot], krt_buf.at[slot], sem.at[1, slot]).wait()

    for k in range(n_pages):
        kc_scr[k * page:(k + 1) * page, :] = kv_buf[slot, k].astype(BF16)
        krt_scr[:, k * page:(k + 1) * page] = krt_buf[slot, k].astype(BF16)

    ql = ql_ref[...]
    qr = qr_ref[...]
    c_exp = sm_scale * LOG2_E

    def tree(op, parts):
        while len(parts) > 1:
            parts = [op(parts[i], parts[i + 1]) for i in range(0, len(parts) - 1, 2)] + parts[len(parts) & ~1:]
        return parts[0]

    def scores(lo, hi):
        s = _dot_nt(ql, kc_scr[lo:hi, :]) + _dot(qr, krt_scr[:, lo:hi])
        return [s[:, i * LANES:(i + 1) * LANES] for i in range(s.shape[-1] // LANES)]

    def weights(s_parts):
        m = jnp.max(tree(jnp.maximum, s_parts), axis=-1, keepdims=True)
        p_parts = [jnp.exp2((sp - m) * c_exp) for sp in s_parts]
        l = jnp.sum(tree(jnp.add, p_parts), axis=-1, keepdims=True)
        return m, l, jnp.concatenate([pp.astype(BF16) for pp in p_parts], axis=-1)

    n_keys = kc_scr.shape[0]
    bounds = [(i * n_keys // SAMPLE_KEY_CHUNKS, (i + 1) * n_keys // SAMPLE_KEY_CHUNKS)
              for i in range(SAMPLE_KEY_CHUNKS)]
    chunks = []
    s_next = scores(*bounds[0])
    for i, (lo, hi) in enumerate(bounds):
        s_cur = s_next
        if i + 1 < SAMPLE_KEY_CHUNKS:
            s_next = scores(*bounds[i + 1])
        m_c, l_c, p_c = weights(s_cur)
        chunks.append((m_c, l_c, _dot(p_c, kc_scr[lo:hi, :])))

    tt = cn_ref.shape[0]
    qlf = ql.astype(F32)
    qrf = qr.astype(F32)
    q_t = lax.broadcasted_iota(jnp.int32, (ql.shape[0], 1), 0) % tt
    c_new = [cn_ref[t:t + 1, :].astype(BF16).astype(F32) for t in range(tt)]
    r_new = [rn_ref[t:t + 1, :].astype(BF16).astype(F32) for t in range(tt)]
    s_new = [jnp.sum(qlf * c_new[t], axis=-1, keepdims=True) + jnp.sum(qrf * r_new[t], axis=-1, keepdims=True)
             for t in range(tt)]

    m = tree(jnp.maximum, [c[0] for c in chunks])
    for t in range(tt):
        m = jnp.where(q_t >= t, jnp.maximum(m, s_new[t]), m)
    l = jnp.zeros_like(m)
    acc = jnp.zeros_like(chunks[0][2])
    for m_c, l_c, acc_c in chunks:
        w = jnp.exp2((m_c - m) * c_exp)
        l = l + w * l_c
        acc = acc + w * acc_c
    for t in range(tt):
        p_t = jnp.where(q_t >= t, jnp.exp2((s_new[t] - m) * c_exp), 0.0)
        l = l + p_t
        acc = acc + p_t * c_new[t]
    o_ref[...] = acc / l


def _attn_sample(ql, qr, c_new, r_new, cache_kv, cache_krt, page_table, j, sm_scale):
    B, rows, kvl = ql.shape
    rope = qr.shape[-1]
    tt = c_new.shape[1]
    page = cache_kv.shape[2]
    n_pages = page_table.shape[1]
    pt = page_table.reshape(-1)

    per_b = lambda w: pl.BlockSpec((None, rows, w), lambda b, pt_ref: (b, 0, 0))
    new_b = lambda w: pl.BlockSpec((None, tt, w), lambda b, pt_ref: (b, 0, 0))
    hbm = pl.BlockSpec(memory_space=pl.ANY)
    return pl.pallas_call(
        functools.partial(_attn_sample_body, sm_scale, j),
        grid_spec=pltpu.PrefetchScalarGridSpec(
            num_scalar_prefetch=1,
            grid=(B,),
            in_specs=[per_b(kvl), per_b(rope), new_b(kvl), new_b(rope), hbm, hbm],
            out_specs=pl.BlockSpec((None, rows, kvl), lambda b, pt_ref: (b, 0, 0)),
            scratch_shapes=[
                pltpu.VMEM((2, n_pages, page, kvl), F32),
                pltpu.VMEM((2, n_pages, rope, page), F32),
                pltpu.VMEM((n_pages * page, kvl), BF16),
                pltpu.VMEM((rope, n_pages * page), BF16),
                pltpu.SemaphoreType.DMA((2, 2)),
            ],
        ),
        out_shape=jax.ShapeDtypeStruct((B, rows, kvl), F32),
        compiler_params=_params("arbitrary"),
        name="attn_sample",
    )(pt, ql, qr, c_new, r_new, cache_kv, cache_krt)


def _mla_out_sample_body(oh_ref, x_ref, post_ref, wuv_ref, wo_ref, o_ref):
    o_heads = [oh_ref[hd] for hd in range(oh_ref.shape[0])]
    o_ref[...] = _mla_out(o_heads, x_ref[...], post_ref[...], wuv_ref, wo_ref)


def _mla_out_sample(oh, x, j, layer, post, wuv, wo):
    n_heads, T, kvl = oh.shape
    D = x.shape[-1]
    vd = wuv.shape[-1]
    return pl.pallas_call(
        _mla_out_sample_body,
        grid=(1,),
        in_specs=[
            pl.BlockSpec((n_heads, T, kvl), lambda g: (0, 0, 0)),
            pl.BlockSpec((T, D), lambda g: (0, 0)),
            pl.BlockSpec((None, 1, D), lambda g: (layer, 0, 0)),
            pl.BlockSpec((None, n_heads, kvl, vd), lambda g: (j, 0, 0, 0)),
            pl.BlockSpec((None, n_heads * vd, D), lambda g: (j, 0, 0)),
        ],
        out_specs=pl.BlockSpec((T, D), lambda g: (0, 0)),
        out_shape=jax.ShapeDtypeStruct((T, D), F32),
        compiler_params=_params("arbitrary"),
        name="mla_out_sample",
    )(oh, x, post, wuv, wo)


def _rope_tables(pos, rope, n_heads):
    half = rope // 2
    inv_freq = 1.0 / (ROPE_THETA ** (jnp.arange(half, dtype=F32) * (2.0 / rope)))
    ang = pos.astype(F32)[:, None] * inv_freq[None, :]
    cos, sin = jnp.cos(ang), jnp.sin(ang)
    cos_k = jnp.concatenate([cos, cos], axis=-1)
    sin_k = jnp.concatenate([-sin, sin], axis=-1)
    return jnp.tile(cos_k, (1, n_heads)), jnp.tile(sin_k, (1, n_heads)), cos_k, sin_k


def kernel(x_prompt, x_sample, state_conv_a, state_conv_b, cache_kv_latent, cache_k_rope, page_table,
           mix_pre_gain, mix_post_gain, ffn_pre_gain, ffn_post_gain, w_ffn_gate, w_ffn_up, w_ffn_down,
           w_in_conv, conv_a_w, conv_a_b, ln_a_gain, ln_a_bias, conv_b_w, w_out_conv,
           w_in_mla, q_norm_gain, w_uq, kv_norm_gain, w_uk, w_uv, w_o_mla):
    Bp, Sp, D = x_prompt.shape
    Bs, Ss, _ = x_sample.shape
    depth = mix_pre_gain.shape[0]
    n_mla = w_in_mla.shape[0]
    kvl, n_heads, nope = w_uk.shape[1:]
    ql = q_norm_gain.shape[-1]
    rope = w_in_mla.shape[-1] - ql - kvl
    sm_scale = float(nope + rope) ** -0.5
    past_len = page_table.shape[1] * cache_kv_latent.shape[2]

    row3 = lambda a: a.reshape(a.shape[0], 1, a.shape[1])
    mix_pre, mix_post = row3(mix_pre_gain), row3(mix_post_gain)
    ffn_pre, ffn_post = row3(ffn_pre_gain), row3(ffn_post_gain)
    wg, wu, wd = w_ffn_gate.astype(BF16), w_ffn_up.astype(BF16), w_ffn_down.astype(BF16)
    win_c, wout_c = w_in_conv.astype(BF16), w_out_conv.astype(BF16)
    ba, lng, lnb = row3(conv_a_b), row3(ln_a_gain), row3(ln_a_bias)
    wa_tiles = jnp.broadcast_to(conv_a_w[:, :, None, :], conv_a_w.shape[:2] + (SUBLANES,) + conv_a_w.shape[2:])
    win_m = w_in_mla.astype(BF16)
    qg, kvg = row3(q_norm_gain), row3(kv_norm_gain)
    wuq4 = w_uq.reshape(n_mla, ql, n_heads, nope + rope)
    wuqn = wuq4[..., :nope].reshape(n_mla, ql, n_heads * nope).astype(BF16)
    wuqr = wuq4[..., nope:].reshape(n_mla, ql, n_heads * rope).astype(BF16)
    wukt = jnp.transpose(w_uk, (0, 2, 3, 1)).astype(BF16)
    wuv = jnp.transpose(w_uv, (0, 2, 1, 3)).astype(BF16)
    wo = w_o_mla.astype(BF16)

    tab_p = _rope_tables(jnp.arange(Sp), rope, n_heads)
    tab_s = _rope_tables(past_len + jnp.repeat(jnp.arange(Ss), Bs), rope, n_heads)

    yp = x_prompt.reshape(Bp * Sp, D)
    ys = jnp.transpose(x_sample, (1, 0, 2))
    cache_krt = jnp.swapaxes(cache_k_rope, 2, 3)
    sa_tm = jnp.transpose(state_conv_a, (0, 2, 1, 3))
    sb_tm = jnp.transpose(state_conv_b, (0, 2, 1, 3))

    ca_p, ca_s, cb_p, cb_s = [], [], [], []
    kv_p, kv_s, kr_p, kr_s = [], [], [], []
    for layer in range(depth):
        if layer % 2 == 0:
            i = layer // 2
            prm = (mix_pre, mix_post, win_c, conv_a_w, ba, lng, lnb, conv_b_w, wout_c)
            yp, na_p, nb_p = _conv_prompt(yp, i, layer, Bp, *prm[:3], wa_tiles, *prm[4:])
            ys, na_s, nb_s = _conv_sample(ys, sa_tm, sb_tm, i, layer, *prm)
            ca_p.append(na_p); cb_p.append(nb_p); ca_s.append(na_s); cb_s.append(nb_s)
        else:
            j = layer // 2
            prm = (mix_pre, win_m, qg, wuqn, wuqr, kvg, wukt)
            qlp, qrp, c_p, r_p, cb16, rb16 = _mla_proj(yp, j, layer, *prm, *tab_p)
            yp = _attn_prompt(qlp, qrp, cb16, rb16, yp, j, layer, Bp, mix_post, wuv, wo, sm_scale)
            ys2 = ys.reshape(Ss * Bs, D)
            qls, qrs, c_s, r_s, _, _ = _mla_proj(ys2, j, layer, *prm, *tab_s)
            to_b = lambda a: jnp.transpose(a.reshape(n_heads, Ss, Bs, a.shape[-1]), (2, 0, 1, 3)).reshape(
                Bs, n_heads * Ss, a.shape[-1])
            c_sb = jnp.transpose(c_s.reshape(Ss, Bs, kvl), (1, 0, 2))
            r_sb = jnp.transpose(r_s.reshape(Ss, Bs, rope), (1, 0, 2))
            o_lat = _attn_sample(to_b(qls), to_b(qrs), c_sb, r_sb, cache_kv_latent, cache_krt,
                                 page_table, j, sm_scale)
            oh = jnp.transpose(o_lat.reshape(Bs, n_heads, Ss, kvl), (1, 2, 0, 3)).reshape(n_heads, Ss * Bs, kvl)
            ys = _mla_out_sample(oh, ys2, j, layer, mix_post, wuv, wo).reshape(Ss, Bs, D)
            kv_p.append(c_p.reshape(Bp, Sp, kvl)); kr_p.append(r_p.reshape(Bp, Sp, rope))
            kv_s.append(c_sb); kr_s.append(r_sb)
        yp = _ffn(yp, layer, ffn_pre, ffn_post, wg, wu, wd)
        ys = _ffn(ys.reshape(Ss * Bs, D), layer, ffn_pre, ffn_post, wg, wu, wd).reshape(Ss, Bs, D)

    from_tm = lambda xs: jnp.transpose(jnp.stack(xs), (0, 2, 1, 3))
    return (yp.reshape(Bp, Sp, D), jnp.transpose(ys, (1, 0, 2)),
            jnp.stack(ca_p), from_tm(ca_s), jnp.stack(cb_p), from_tm(cb_s),
            jnp.stack(kv_p), jnp.stack(kv_s), jnp.stack(kr_p), jnp.stack(kr_s))
```
